```python
import math
import jax, jax.numpy as jnp
from jax import lax
import numpy as np

D_MODEL = 2048
BATCH = 1
SEQ = 8192
DEPTH = 4

GRID_W = 64
CTX_LEN = 256
EPS = 1e-6
ROPE_BASE = 10000.0
N_BRANCH = 4
BRANCH_W = 512
A_HEADS = 8
A_KV_HEADS = 2
A_HEAD_DIM = 64
WINDOW = 128
A_BLOCK = 128
B_HEADS = 4
B_NOPE = 128
B_ROPE = 64
B_VDIM = 128
B_Q_LORA = 384
B_KV_LORA = 128
B_BLOCK = 128
C_HEADS = 4
C_DK = 128
C_DV = 128
C_CHUNK = 16
POOL_WINDOWS = (2, 4, 8, 16)
POOL_GROUP = BRANCH_W // len(POOL_WINDOWS)

A_K0 = 0
A_V0 = A_K0 + A_KV_HEADS * A_HEAD_DIM
B_CKV0 = A_V0 + A_KV_HEADS * A_HEAD_DIM
B_KR0 = B_CKV0 + B_KV_LORA
C_FF0 = B_KR0 + B_ROPE
C_FB0 = C_FF0 + C_HEADS * C_DK
C_I0 = C_FB0 + C_HEADS * C_DK
N_KV_COLS = C_I0 + C_HEADS * C_DV
A_Q0 = N_KV_COLS
B_CQ0 = A_Q0 + A_HEADS * A_HEAD_DIM
C_Q0 = B_CQ0 + B_Q_LORA
D_X0 = C_Q0 + C_HEADS * C_DK
GATE0 = D_X0 + BRANCH_W
MERGE0 = GATE0 + N_BRANCH * BRANCH_W
N_IN_COLS = MERGE0 + N_BRANCH * D_MODEL

kernel_name = "hybrid_parallel_mixer_dit_trunk"


def rms_norm(x, g):
    xf = x.astype(jnp.float32)
    y = xf * lax.rsqrt(jnp.mean(xf * xf, axis=-1, keepdims=True) + EPS)
    return (y * g.astype(jnp.float32)).astype(x.dtype)


def rope_1d(x, pos):
    half = x.shape[-1] // 2
    freqs = ROPE_BASE ** (-jnp.arange(half, dtype=jnp.float32) / half)
    ang = pos.astype(jnp.float32)[:, None] * freqs[None, :]
    cos = jnp.cos(ang)[None, :, None, :]
    sin = jnp.sin(ang)[None, :, None, :]
    xf = x.astype(jnp.float32)
    x1, x2 = xf[..., :half], xf[..., half:]
    return jnp.concatenate([x1 * cos - x2 * sin, x1 * sin + x2 * cos], axis=-1).astype(x.dtype)


def rope_2d(x, rows, cols):
    d = x.shape[-1] // 2
    return jnp.concatenate([rope_1d(x[..., :d], rows), rope_1d(x[..., d:], cols)], axis=-1)


def window_gqa(q, k, v, kc, vc, sink):
    B, T = q.shape[:2]
    N = T // A_BLOCK
    G = A_HEADS // A_KV_HEADS
    qb = q.reshape(B, N, A_BLOCK, A_KV_HEADS, G, A_HEAD_DIM)
    pad = ((0, 0), (A_BLOCK, A_BLOCK), (0, 0), (0, 0))
    kp = jnp.pad(k, pad).reshape(B, N + 2, A_BLOCK, A_KV_HEADS, A_HEAD_DIM)
    vp = jnp.pad(v, pad).reshape(B, N + 2, A_BLOCK, A_KV_HEADS, A_HEAD_DIM)
    kb = jnp.concatenate([kp[:, :-2], kp[:, 1:-1], kp[:, 2:]], axis=2)
    vb = jnp.concatenate([vp[:, :-2], vp[:, 1:-1], vp[:, 2:]], axis=2)
    scale = A_HEAD_DIM ** -0.5
    s_loc = jnp.einsum('bnqhgd,bnkhd->bnhgqk', qb, kb, preferred_element_type=jnp.float32) * scale
    qi = jnp.arange(A_BLOCK)[:, None]
    kj = jnp.arange(3 * A_BLOCK)[None, :] - A_BLOCK
    in_win = jnp.abs(kj - qi) <= WINDOW
    kpos = jnp.arange(N)[:, None] * A_BLOCK + kj
    valid = (kpos >= 0) & (kpos < T)
    mask = in_win[None] & valid[:, None, :]
    s_loc = jnp.where(mask[None, :, None, None], s_loc, -jnp.inf)
    s_ctx = jnp.einsum('bnqhgd,bchd->bnhgqc', qb, kc, preferred_element_type=jnp.float32) * scale
    s_sink = jnp.broadcast_to(sink.astype(jnp.float32).reshape(A_KV_HEADS, G)[None, None, :, :, None, None],
                              s_loc.shape[:-1] + (1,))
    p = jax.nn.softmax(jnp.concatenate([s_loc, s_ctx, s_sink], axis=-1), axis=-1).astype(v.dtype)
    L = kc.shape[1]
    o = (jnp.einsum('bnhgqk,bnkhd->bnqhgd', p[..., :3 * A_BLOCK], vb)
         + jnp.einsum('bnhgqc,bchd->bnqhgd', p[..., 3 * A_BLOCK:3 * A_BLOCK + L], vc))
    return o.reshape(B, T, A_HEADS * A_HEAD_DIM)


def ctx_gqa(qc, kc, vc, sink):
    B, L = qc.shape[:2]
    G = A_HEADS // A_KV_HEADS
    qg = qc.reshape(B, L, A_KV_HEADS, G, A_HEAD_DIM)
    s = jnp.einsum('blhgd,bmhd->bhglm', qg, kc, preferred_element_type=jnp.float32) * A_HEAD_DIM ** -0.5
    s_sink = jnp.broadcast_to(sink.astype(jnp.float32).reshape(A_KV_HEADS, G)[None, :, :, None, None],
                              s.shape[:-1] + (1,))
    p = jax.nn.softmax(jnp.concatenate([s, s_sink], axis=-1), axis=-1).astype(vc.dtype)
    o = jnp.einsum('bhglm,bmhd->blhgd', p[..., :L], vc)
    return o.reshape(B, L, A_HEADS * A_HEAD_DIM)


def branch_window_attn(p, pc, rows, cols, sink, need_ctx):
    B, T = p.shape[:2]
    L = pc.shape[1]
    q = rope_2d(p[..., A_Q0:A_Q0 + A_HEADS * A_HEAD_DIM].reshape(B, T, A_HEADS, A_HEAD_DIM), rows, cols)
    k = rope_2d(p[..., A_K0:A_V0].reshape(B, T, A_KV_HEADS, A_HEAD_DIM), rows, cols)
    v = p[..., A_V0:B_CKV0].reshape(B, T, A_KV_HEADS, A_HEAD_DIM)
    kc = pc[..., A_K0:A_V0].reshape(B, L, A_KV_HEADS, A_HEAD_DIM)
    vc = pc[..., A_V0:B_CKV0].reshape(B, L, A_KV_HEADS, A_HEAD_DIM)
    o = window_gqa(q, k, v, kc, vc, sink)
    oc = None
    if need_ctx:
        qc = pc[..., A_Q0:A_Q0 + A_HEADS * A_HEAD_DIM].reshape(B, L, A_HEADS, A_HEAD_DIM)
        oc = ctx_gqa(qc, kc, vc, sink)
    return o, oc


def mla_kv(pp, kv_norm, w_ukv):
    B, T = pp.shape[:2]
    ckv = rms_norm(pp[..., B_CKV0:B_CKV0 + B_KV_LORA], kv_norm)
    kv = (ckv @ w_ukv).reshape(B, T, B_HEADS, B_NOPE + B_VDIM)
    k_rope = pp[..., B_KR0:B_KR0 + B_ROPE][:, :, None, :]
    return kv[..., :B_NOPE], kv[..., B_NOPE:], k_rope


def mla_q(pp, q_norm, w_uq):
    B, T = pp.shape[:2]
    cq = rms_norm(pp[..., B_CQ0:B_CQ0 + B_Q_LORA], q_norm)
    q = (cq @ w_uq).reshape(B, T, B_HEADS, B_NOPE + B_ROPE)
    return q[..., :B_NOPE], q[..., B_NOPE:]


def mla_keys(k_nope, k_rope):
    return jnp.concatenate([k_nope, jnp.broadcast_to(k_rope, k_nope.shape[:-1] + (B_ROPE,))], axis=-1)


def branch_mla(p, pc, rows, cols, q_norm, w_uq, kv_norm, w_ukv, need_ctx):
    B, T = p.shape[:2]
    scale = (B_NOPE + B_ROPE) ** -0.5
    k_nope, v, k_rope = mla_kv(p, kv_norm, w_ukv)
    k = mla_keys(k_nope, rope_2d(k_rope, rows, cols))
    q_nope, q_rope = mla_q(p, q_norm, w_uq)
    q = jnp.concatenate([q_nope, rope_2d(q_rope, rows, cols)], axis=-1)
    kc_nope, vc, kc_rope = mla_kv(pc, kv_norm, w_ukv)
    kc = mla_keys(kc_nope, kc_rope)
    k_all = jnp.concatenate([kc, k], axis=1)
    v_all = jnp.concatenate([vc, v], axis=1)
    N = T // B_BLOCK
    qb = jnp.moveaxis(q.reshape(B, N, B_BLOCK, B_HEADS, B_NOPE + B_ROPE), 1, 0)

    def block(qi):
        s = jnp.einsum('bqhd,bkhd->bhqk', qi, k_all, preferred_element_type=jnp.float32) * scale
        pr = jax.nn.softmax(s, axis=-1).astype(v_all.dtype)
        return jnp.einsum('bhqk,bkhd->bqhd', pr, v_all)

    o = jnp.moveaxis(lax.map(block, qb), 0, 1).reshape(B, T, B_HEADS * B_VDIM)
    oc = None
    if need_ctx:
        L = pc.shape[1]
        qc_nope, qc_rope = mla_q(pc, q_norm, w_uq)
        qc = jnp.concatenate([qc_nope, qc_rope], axis=-1)
        s = jnp.einsum('blhd,bmhd->bhlm', qc, kc, preferred_element_type=jnp.float32) * scale
        pr = jax.nn.softmax(s, axis=-1).astype(vc.dtype)
        oc = jnp.einsum('bhlm,bmhd->blhd', pr, vc).reshape(B, L, B_HEADS * B_VDIM)
    return o, oc


def hgrn_gates(z, lb):
    zf = z.astype(jnp.float32)
    log_f = jnp.logaddexp(jnp.log(lb), jnp.log1p(-lb) + jax.nn.log_sigmoid(zf))
    k = (1.0 - lb) * jax.nn.sigmoid(-zf)
    return k, log_f


def to_chunks(a):
    B, T = a.shape[:2]
    return a.reshape((B, T // C_CHUNK, C_CHUNK) + a.shape[2:])


def chunk_states(k, v, b, s0):
    b_last = b[:, :, -1]
    u = jnp.einsum('bnchk,bnchv->bnhkv', k * jnp.exp(b_last[:, :, None] - b), v)
    a = jnp.exp(b_last)

    def step(s, av):
        a_n, u_n = av
        return a_n[..., None] * s + u_n, s

    s_fin, s_start = lax.scan(step, s0, (jnp.moveaxis(a, 1, 0), jnp.moveaxis(u, 1, 0)))
    return s_start, s_fin


def gla_final_state(k, v, log_f, s0):
    kc, vc = to_chunks(k), to_chunks(v)
    b = jnp.cumsum(to_chunks(log_f), axis=2)
    return chunk_states(kc, vc, b, s0)[1]


def gla_chunkwise(q, k, v, log_f, s0):
    B, T, H, _ = q.shape
    qc = to_chunks(q * C_DK ** -0.5)
    kc, vc = to_chunks(k), to_chunks(v)
    b = jnp.cumsum(to_chunks(log_f), axis=2)
    s_start, s_fin = chunk_states(kc, vc, b, s0)
    idx = jnp.arange(C_CHUNK)
    lower = (idx[:, None] >= idx[None, :])[None, None, :, :, None, None]
    decay = jnp.exp(jnp.where(lower, b[:, :, :, None] - b[:, :, None, :], -jnp.inf))
    att = jnp.einsum('bnihk,bnijhk,bnjhk->bnhij', qc, decay, kc)
    o = (jnp.einsum('bnhij,bnjhv->bnihv', att, vc)
         + jnp.einsum('bnihk,nbhkv->bnihv', qc * jnp.exp(b), s_start))
    return o.reshape(B, T, H, v.shape[-1]), s_fin


def hgrn_inputs(pp, lb_f, lb_b):
    B, T = pp.shape[:2]
    v = pp[..., C_I0:N_KV_COLS].astype(jnp.float32).reshape(B, T, C_HEADS, C_DV)
    k_f, lf_f = hgrn_gates(pp[..., C_FF0:C_FB0], lb_f)
    k_b, lf_b = hgrn_gates(pp[..., C_FB0:C_I0], lb_b)
    shp = (B, T, C_HEADS, C_DK)
    return v, k_f.reshape(shp), lf_f.reshape(shp), k_b.reshape(shp), lf_b.reshape(shp)


def flip(a):
    return jnp.flip(a, axis=1)


def branch_hgrn(p, pc, lb_f, lb_b, norm_g, need_ctx):
    B, T = p.shape[:2]
    L = pc.shape[1]
    s0 = jnp.zeros((B, C_HEADS, C_DK, C_DV), jnp.float32)
    v, k_f, lf_f, k_b, lf_b = hgrn_inputs(p, lb_f, lb_b)
    q = jax.nn.silu(p[..., C_Q0:D_X0].astype(jnp.float32)).reshape(B, T, C_HEADS, C_DK)
    vc, kc_f, lfc_f, kc_b, lfc_b = hgrn_inputs(pc, lb_f, lb_b)
    oc = None
    if need_ctx:
        qc = jax.nn.silu(pc[..., C_Q0:D_X0].astype(jnp.float32)).reshape(B, L, C_HEADS, C_DK)
        oc_f, sc_f = gla_chunkwise(qc, kc_f, vc, lfc_f, s0)
        oc_b, sc_b = gla_chunkwise(flip(qc), flip(kc_b), flip(vc), flip(lfc_b), s0)
        oc = rms_norm(oc_f + flip(oc_b), norm_g.reshape(C_HEADS, C_DV)).reshape(B, L, C_HEADS * C_DV).astype(p.dtype)
    else:
        sc_f = gla_final_state(kc_f, vc, lfc_f, s0)
        sc_b = gla_final_state(flip(kc_b), flip(vc), flip(lfc_b), s0)
    o_f, _ = gla_chunkwise(q, k_f, v, lf_f, sc_f)
    o_b, _ = gla_chunkwise(flip(q), flip(k_b), flip(v), flip(lf_b), sc_b)
    o = rms_norm(o_f + flip(o_b), norm_g.reshape(C_HEADS, C_DV)).reshape(B, T, C_HEADS * C_DV).astype(p.dtype)
    return o, oc


def multiscale_pool(u, w_pool, pool_scale):
    B, T, W = u.shape
    uf = u.astype(jnp.float32)
    csum = jnp.concatenate([jnp.zeros((B, 1, W), jnp.float32), jnp.cumsum(uf, axis=1)], axis=1)
    t = jnp.arange(T)
    outs = []
    for g, w in enumerate(POOL_WINDOWS):
        lo = jnp.clip(t - w // 2, 0, T)
        hi = jnp.clip(t + w - w // 2, 0, T)
        sl = slice(g * POOL_GROUP, (g + 1) * POOL_GROUP)
        mean = (csum[:, hi][..., sl] - csum[:, lo][..., sl]) / (hi - lo).astype(jnp.float32)[None, :, None]
        d = (mean - uf[..., sl]).astype(u.dtype)
        outs.append(d @ w_pool[g])
    return jnp.concatenate(outs, axis=-1) * pool_scale


def branch_pool(p, pc, w_pool, pool_scale, need_ctx):
    o = multiscale_pool(p[..., D_X0:GATE0], w_pool, pool_scale)
    oc = multiscale_pool(pc[..., D_X0:GATE0], w_pool, pool_scale) if need_ctx else None
    return o, oc


def merge(pp, outs, w_branch, w_out):
    B, T = pp.shape[:2]
    gates = pp[..., GATE0:MERGE0].reshape(B, T, N_BRANCH, BRANCH_W)
    mg = pp[..., MERGE0:N_IN_COLS].reshape(B, T, N_BRANCH, D_MODEL)
    ys = jnp.stack(outs, axis=2) * jax.nn.silu(gates)
    yb = jnp.einsum('btnw,nwd->btnd', ys, w_branch)
    merged = jnp.sum(jax.nn.sigmoid(mg) * yb, axis=2)
    return merged @ w_out


def setup_inputs(seed: int = 0) -> dict:
    key = jax.random.key(seed)
    ks = jax.random.split(key, 20)
    f32 = jnp.float32

    def nrm(k, shape, s):
        return jax.random.normal(k, shape, f32) * s

    return {
        "x": nrm(ks[0], (BATCH, SEQ, D_MODEL), 1.0),
        "c": nrm(ks[1], (BATCH, D_MODEL), 1.0),
        "ctx": nrm(ks[2], (BATCH, CTX_LEN, D_MODEL), 1.0),
        "c_ctx": nrm(ks[3], (D_MODEL,), 1.0),
        "w_ada": nrm(ks[4], (DEPTH, D_MODEL, 3 * D_MODEL), 0.5 * D_MODEL ** -0.5),
        "b_ada": nrm(ks[5], (DEPTH, 3 * D_MODEL), 0.02),
        "g_pre": 1.0 + nrm(ks[6], (DEPTH, D_MODEL), 0.05),
        "g_post": 1.0 + nrm(ks[7], (DEPTH, D_MODEL), 0.05),
        "w_in": nrm(ks[8], (DEPTH, D_MODEL, N_IN_COLS), D_MODEL ** -0.5),
        "a_sink": nrm(ks[9], (DEPTH, A_HEADS), 0.5),
        "mla_q_norm": 1.0 + nrm(ks[10], (DEPTH, B_Q_LORA), 0.05),
        "w_uq": nrm(ks[11], (DEPTH, B_Q_LORA, B_HEADS * (B_NOPE + B_ROPE)), B_Q_LORA ** -0.5),
        "mla_kv_norm": 1.0 + nrm(ks[12], (DEPTH, B_KV_LORA), 0.05),
        "w_ukv": nrm(ks[13], (DEPTH, B_KV_LORA, B_HEADS * (B_NOPE + B_VDIM)), B_KV_LORA ** -0.5),
        "hgrn_lb": nrm(ks[14], (DEPTH, 2, C_HEADS * C_DK), 1.0),
        "hgrn_norm": 1.0 + nrm(ks[15], (DEPTH, C_HEADS * C_DV), 0.05),
        "w_pool": nrm(ks[16], (DEPTH, len(POOL_WINDOWS), POOL_GROUP, POOL_GROUP), POOL_GROUP ** -0.5),
        "pool_scale": 1.0 + nrm(ks[17], (DEPTH, BRANCH_W), 0.1),
        "w_branch": nrm(ks[18], (DEPTH, N_BRANCH, BRANCH_W, D_MODEL), BRANCH_W ** -0.5),
        "w_out": nrm(ks[19], (DEPTH, D_MODEL, D_MODEL), D_MODEL ** -0.5),
    }


def reference(x, c, ctx, c_ctx, w_ada, b_ada, g_pre, g_post, w_in, a_sink, mla_q_norm, w_uq,
              mla_kv_norm, w_ukv, hgrn_lb, hgrn_norm, w_pool, pool_scale, w_branch, w_out):
    T = x.shape[1]
    ROWS = T // GRID_W
    rows = jnp.repeat(jnp.arange(ROWS, dtype=jnp.int32), GRID_W)
    cols = jnp.tile(jnp.arange(GRID_W, dtype=jnp.int32), ROWS)
    lb_all = jnp.cumsum(jax.nn.softmax(hgrn_lb.astype(jnp.float32), axis=0), axis=0)
    lb_all = lb_all - lb_all[:1]
    xc = ctx
    for li in range(DEPTH):
        need_ctx = li < DEPTH - 1
        shift, scale, gate = jnp.split(jax.nn.silu(c) @ w_ada[li] + b_ada[li], 3, axis=-1)
        shift_c, scale_c, gate_c = jnp.split(jax.nn.silu(c_ctx) @ w_ada[li] + b_ada[li], 3, axis=-1)
        h = rms_norm(x, g_pre[li]) * (1.0 + scale[:, None]) + shift[:, None]
        hc = rms_norm(xc, g_pre[li]) * (1.0 + scale_c) + shift_c
        p = h @ w_in[li]
        pc = hc @ (w_in[li] if need_ctx else w_in[li][:, :N_KV_COLS])
        oa, oa_c = branch_window_attn(p, pc, rows, cols, a_sink[li], need_ctx)
        ob, ob_c = branch_mla(p, pc, rows, cols, mla_q_norm[li], w_uq[li], mla_kv_norm[li], w_ukv[li], need_ctx)
        oh, oh_c = branch_hgrn(p, pc, lb_all[li, 0], lb_all[li, 1], hgrn_norm[li], need_ctx)
        od, od_c = branch_pool(p, pc, w_pool[li], pool_scale[li], need_ctx)
        y = merge(p, (oa, ob, oh, od), w_branch[li], w_out[li])
        x = x + gate[:, None] * rms_norm(y, g_post[li])
        if need_ctx:
            yc = merge(pc, (oa_c, ob_c, oh_c, od_c), w_branch[li], w_out[li])
            xc = xc + gate_c * rms_norm(yc, g_post[li])
    return x
```

```python
import functools

import jax
import jax.numpy as jnp
from jax import lax
from jax.experimental import pallas as pl
from jax.experimental.pallas import tpu as pltpu

F32 = jnp.float32
BF16 = jnp.bfloat16

D_MODEL = 2048
DEPTH = 4
GRID_W = 64
EPS = 1e-6
ROPE_BASE = 10000.0
N_BRANCH = 4
BRANCH_W = 512
A_HEADS = 8
A_HEAD_DIM = 64
WINDOW = 128
B_HEADS = 4
B_NOPE = 128
B_ROPE = 64
B_VDIM = 128
B_Q_LORA = 384
B_KV_LORA = 128
C_HEADS = 4
C_DK = 128
POOL_WINDOWS = (2, 4, 8, 16)

O_CFF = 448
O_CQ_END = 2880
O_COLS = 14144
P_KV = 0
P_CFF = 512
P_CFB = 1024
P_CI = 1536
P_AQ = 2048
P_CQ = 2560
P_HQ = 3072
P_DX = 3584
P_GATE = 4096
P_MERGE = 6144
P_COLS = 14336

BLK = 128
LANES = 128
VMEM_LIMIT = 56 * 1024 * 1024

NT_DIMS = (((1,), (1,)), ((), ()))


def _cparams(sem):
    return pltpu.CompilerParams(dimension_semantics=sem, vmem_limit_bytes=VMEM_LIMIT)


def _pick_tile(n, cap, mult=BLK):
    best = mult
    t = mult
    while t <= min(n, cap):
        if n % t == 0:
            best = t
        t += mult
    return best


def _dot(a, b):
    return jnp.dot(a, b, preferred_element_type=F32)


def _dot_nt(a, b):
    return lax.dot_general(a, b, NT_DIMS, preferred_element_type=F32)


def _sigmoid(x):
    return 1.0 / (1.0 + jnp.exp(-x))


def _ada_kernel(cc_ref, w_ref, b_ref, o_ref):
    cc = cc_ref[...]
    s = (cc * _sigmoid(cc)).astype(BF16)
    o_ref[...] = _dot(s, w_ref[...].astype(BF16)) + b_ref[...]


def _ada_all(cc, w_ada, b_ada):
    tn = 1024
    n3 = 3 * D_MODEL
    return pl.pallas_call(
        _ada_kernel,
        grid=(DEPTH, n3 // tn),
        in_specs=[
            pl.BlockSpec((8, D_MODEL), lambda l, j: (0, 0)),
            pl.BlockSpec((None, D_MODEL, tn), lambda l, j: (l, 0, j)),
            pl.BlockSpec((None, 1, tn), lambda l, j: (l, 0, j)),
        ],
        out_specs=pl.BlockSpec((None, 8, tn), lambda l, j: (l, 0, j)),
        out_shape=jax.ShapeDtypeStruct((DEPTH, 8, n3), F32),
        compiler_params=_cparams(("arbitrary", "arbitrary")),
        name="adaln",
    )(cc, w_ada, b_ada.reshape(DEPTH, 1, n3))


def _lb_kernel(x_ref, o_ref):
    x = x_ref[...]
    m = jnp.max(x, axis=0, keepdims=True)
    e = jnp.exp(x - m)
    sm = e / jnp.sum(e, axis=0, keepdims=True)
    rows = []
    run = sm[0:1]
    first = run
    for l in range(DEPTH):
        if l > 0:
            run = run + sm[l:l + 1]
        rows.append(run - first)
    o_ref[...] = jnp.concatenate(rows, axis=0)


def _lower_bounds(hgrn_lb):
    n = 2 * C_HEADS * C_DK
    out = pl.pallas_call(
        _lb_kernel,
        out_shape=jax.ShapeDtypeStruct((DEPTH, n), F32),
        name="hgrn_lb",
    )(hgrn_lb.reshape(DEPTH, n).astype(F32))
    return out.reshape(DEPTH, 2, 1, C_HEADS * C_DK)


def _rms(x, g):
    return x * lax.rsqrt(jnp.mean(x * x, axis=-1, keepdims=True) + EPS) * g


def _mod_rows(mod_ref, is_ctx):
    m = mod_ref[...]
    sel = jnp.where(is_ctx, m[1:2, :], m[0:1, :])
    return sel[:, 0:D_MODEL], sel[:, D_MODEL:2 * D_MODEL], sel[:, 2 * D_MODEL:3 * D_MODEL]


def _prenorm_kernel(x_ref, g_ref, mod_ref, h_ref, *, ctx_tiles):
    is_ctx = pl.program_id(0) < ctx_tiles
    shift, scale, _ = _mod_rows(mod_ref, is_ctx)
    h = _rms(x_ref[...], g_ref[...]) * (1.0 + scale) + shift
    h_ref[...] = h.astype(BF16)


def _prenorm(xs, g_pre, mod, li, L):
    R = xs.shape[0]
    tm = 256
    return pl.pallas_call(
        functools.partial(_prenorm_kernel, ctx_tiles=L // tm),
        grid=(R // tm,),
        in_specs=[
            pl.BlockSpec((tm, D_MODEL), lambda i: (i, 0)),
            pl.BlockSpec((None, 1, D_MODEL), lambda i: (li, 0, 0)),
            pl.BlockSpec((None, 8, 3 * D_MODEL), lambda i: (li, 0, 0)),
        ],
        out_specs=pl.BlockSpec((tm, D_MODEL), lambda i: (i, 0)),
        out_shape=jax.ShapeDtypeStruct((R, D_MODEL), BF16),
        compiler_params=_cparams(("arbitrary",)),
        name="prenorm",
    )(xs, g_pre, mod)


def _proj_kernel(h_ref, w_ref, o_ref):
    o_ref[...] = _dot(h_ref[...], w_ref[...]).astype(o_ref.dtype)


def _proj(h, w_in_p, li):
    R = h.shape[0]
    tm = _pick_tile(R, 1408)
    tn = 1024
    return pl.pallas_call(
        _proj_kernel,
        grid=(R // tm, P_COLS // tn),
        in_specs=[
            pl.BlockSpec((tm, D_MODEL), lambda i, j: (i, 0)),
            pl.BlockSpec((None, D_MODEL, tn), lambda i, j: (li, 0, j)),
        ],
        out_specs=pl.BlockSpec((tm, tn), lambda i, j: (i, j)),
        out_shape=jax.ShapeDtypeStruct((R, P_COLS), BF16),
        compiler_params=_cparams(("arbitrary", "arbitrary")),
        name="proj",
    )(h, w_in_p)


def _rope_tile(t, cos, sin):
    lane = lax.broadcasted_iota(jnp.int32, t.shape, 1)
    first = (lane & 16) == 0
    partner = jnp.where(first, pltpu.roll(t, LANES - 16, 1), pltpu.roll(t, 16, 1))
    return t * cos + partner * sin


def _prep_kernel(aq_ref, kv_ref, cq_ref, cos_ref, sin_ref, qn_ref, wuq_ref, kn_ref, wukv_ref,
                 aq_o, ak_o, q_o, k_o, v_o):
    cos = cos_ref[...]
    sin = sin_ref[...]
    a_scale = A_HEAD_DIM ** -0.5
    for t in range(A_HEADS * A_HEAD_DIM // LANES):
        sl = slice(t * LANES, (t + 1) * LANES)
        aq_o[:, sl] = (_rope_tile(aq_ref[:, sl].astype(F32), cos, sin) * a_scale).astype(BF16)
    ak_o[...] = _rope_tile(kv_ref[:, 0:128].astype(F32), cos, sin).astype(BF16)

    ckv = kv_ref[:, 256:384].astype(F32)
    ckv = _rms(ckv, kn_ref[...])
    kv = _dot(ckv.astype(BF16), wukv_ref[...])
    kr = _rope_tile(kv_ref[:, 384:512].astype(F32), cos, sin).astype(BF16)
    for h in range(B_HEADS):
        k_o[h, :, 0:128] = kv[:, 256 * h:256 * h + 128].astype(BF16)
        k_o[h, :, 128:256] = kr
        v_o[h] = kv[:, 256 * h + 128:256 * h + 256].astype(BF16)

    cq = cq_ref[...].astype(F32)
    cq = cq * lax.rsqrt(jnp.sum(cq * cq, axis=-1, keepdims=True) * (1.0 / B_Q_LORA) + EPS) * qn_ref[...]
    q = _dot(cq.astype(BF16), wuq_ref[...])
    b_scale = (B_NOPE + B_ROPE) ** -0.5
    for h in range(B_HEADS):
        q_o[h, :, 0:128] = (q[:, 256 * h:256 * h + 128] * b_scale).astype(BF16)
        q_o[h, :, 128:256] = (_rope_tile(q[:, 256 * h + 128:256 * h + 256], cos, sin) * b_scale).astype(BF16)


def _prep(p, cos, sin, qn, wuq, kn, wukv, li):
    R = p.shape[0]
    tm = 256
    row = lambda i: (i, 0)
    return pl.pallas_call(
        _prep_kernel,
        grid=(R // tm,),
        in_specs=[
            pl.BlockSpec((tm, 512), lambda i: (i, P_AQ // 512)),
            pl.BlockSpec((tm, 512), lambda i: (i, P_KV // 512)),
            pl.BlockSpec((tm, 512), lambda i: (i, P_CQ // 512)),
            pl.BlockSpec((tm, LANES), row),
            pl.BlockSpec((tm, LANES), row),
            pl.BlockSpec((None, 1, 512), lambda i: (li, 0, 0)),
            pl.BlockSpec((None, 512, 1024), lambda i: (li, 0, 0)),
            pl.BlockSpec((None, 1, B_KV_LORA), lambda i: (li, 0, 0)),
            pl.BlockSpec((None, B_KV_LORA, 1024), lambda i: (li, 0, 0)),
        ],
        out_specs=[
            pl.BlockSpec((tm, 512), row),
            pl.BlockSpec((tm, LANES), row),
            pl.BlockSpec((B_HEADS, tm, 256), lambda i: (0, i, 0)),
            pl.BlockSpec((B_HEADS, tm, 256), lambda i: (0, i, 0)),
            pl.BlockSpec((B_HEADS, tm, B_VDIM), lambda i: (0, i, 0)),
        ],
        out_shape=[
            jax.ShapeDtypeStruct((R, 512), BF16),
            jax.ShapeDtypeStruct((R, LANES), BF16),
            jax.ShapeDtypeStruct((B_HEADS, R, 256), BF16),
            jax.ShapeDtypeStruct((B_HEADS, R, 256), BF16),
            jax.ShapeDtypeStruct((B_HEADS, R, B_VDIM), BF16),
        ],
        compiler_params=_cparams(("arbitrary",)),
        name="prep",
    )(p, p, p, cos, sin, qn, wuq, kn, wukv)


def _swap_halves(t):
    return jnp.concatenate([t[:, 64:], t[:, :64]], axis=1)


def _win_kernel(sink_ref, q_ref, kp_ref, ks_ref, kn_ref, vp_ref, vs_ref, vn_ref, kc_ref, vc_ref, o_ref,
                *, L, R):
    i = pl.program_id(0)
    q = q_ref[...]
    kl = jnp.concatenate([kp_ref[...], ks_ref[...], kn_ref[...]], axis=0)
    vl = jnp.concatenate([vp_ref[...], vs_ref[...], vn_ref[...]], axis=0)
    kc = kc_ref[...]
    vc = vc_ref[...]
    lo = lax.broadcasted_iota(jnp.int32, (BLK, LANES), 1) < 64
    zero = jnp.zeros((BLK, LANES), BF16)

    rows = 4 * BLK
    rr = lax.broadcasted_iota(jnp.int32, (rows, 3 * BLK), 0)
    cc = lax.broadcasted_iota(jnp.int32, (rows, 3 * BLK), 1)
    qpos = (rr & (BLK - 1)) + i * BLK
    kpos = cc + (i - 1) * BLK
    ok = ((kpos >= L) & (kpos < R) & (qpos >= L)
          & (kpos - qpos <= WINDOW) & (qpos - kpos <= WINDOW))
    hrow = lax.broadcasted_iota(jnp.int32, (rows, 1), 0) // BLK

    for g in range(2):
        t0 = q[:, 256 * g:256 * g + 128]
        t1 = q[:, 256 * g + 128:256 * g + 256]
        if g == 0:
            parts = [jnp.where(lo, t0, zero), jnp.where(lo, _swap_halves(t0), zero),
                     jnp.where(lo, t1, zero), jnp.where(lo, _swap_halves(t1), zero)]
        else:
            parts = [jnp.where(lo, zero, _swap_halves(t0)), jnp.where(lo, zero, t0),
                     jnp.where(lo, zero, _swap_halves(t1)), jnp.where(lo, zero, t1)]
        qs = jnp.concatenate(parts, axis=0)
        s_l = jnp.where(ok, _dot_nt(qs, kl), -jnp.inf)
        s_c = _dot_nt(qs, kc)
        sink = jnp.zeros((rows, 1), F32)
        for j in range(4):
            sink = jnp.where(hrow == j, sink_ref[4 * g + j], sink)
        m = jnp.maximum(jnp.maximum(jnp.max(s_l, axis=-1, keepdims=True),
                                    jnp.max(s_c, axis=-1, keepdims=True)), sink)
        p_l = jnp.exp(s_l - m)
        p_c = jnp.exp(s_c - m)
        den = (jnp.sum(p_l, axis=-1, keepdims=True) + jnp.sum(p_c, axis=-1, keepdims=True)
               + jnp.exp(sink - m))
        o = (_dot(p_l.astype(BF16), vl) + _dot(p_c.astype(BF16), vc)) / den
        for pair in range(2):
            a = o[2 * pair * BLK:(2 * pair + 1) * BLK]
            b = o[(2 * pair + 1) * BLK:(2 * pair + 2) * BLK]
            if g == 0:
                tile = jnp.where(lo, a, _swap_halves(b))
            else:
                tile = jnp.where(lo, _swap_halves(a), b)
            c0 = 256 * g + 128 * pair
            o_ref[:, c0:c0 + 128] = tile.astype(BF16)


def _window_attn(sink, aq, ak, p, L):
    R = aq.shape[0]
    nb = R // BLK
    prev = lambda c: (lambda i: (jnp.maximum(i - 1, 0), c))
    this = lambda c: (lambda i: (i, c))
    nxt = lambda c: (lambda i: (jnp.minimum(i + 1, nb - 1), c))
    return pl.pallas_call(
        functools.partial(_win_kernel, L=L, R=R),
        grid=(nb,),
        in_specs=[
            pl.BlockSpec(memory_space=pltpu.SMEM),
            pl.BlockSpec((BLK, 512), lambda i: (i, 0)),
            pl.BlockSpec((BLK, LANES), prev(0)),
            pl.BlockSpec((BLK, LANES), this(0)),
            pl.BlockSpec((BLK, LANES), nxt(0)),
            pl.BlockSpec((BLK, LANES), prev(1)),
            pl.BlockSpec((BLK, LANES), this(1)),
            pl.BlockSpec((BLK, LANES), nxt(1)),
            pl.BlockSpec((L, LANES), lambda i: (0, 0)),
            pl.BlockSpec((L, LANES), lambda i: (0, 1)),
        ],
        out_specs=pl.BlockSpec((BLK, 512), lambda i: (i, 0)),
        out_shape=jax.ShapeDtypeStruct((R, 512), BF16),
        compiler_params=_cparams(("arbitrary",)),
        name="window_attn",
    )(sink, aq, ak, ak, ak, p, p, p, ak, p)


def _mla_kernel(q_ref, k_ref, v_ref, o_ref, *, L, R, tq, tk):
    i = pl.program_id(1)
    q = q_ref[...]
    n_chunks = jnp.where(i < L // tq, L // tk, R // tk)

    def body(c, carry):
        m, l, acc = carry
        off = pl.multiple_of(c * tk, tk)
        k = k_ref[pl.ds(off, tk), :]
        v = v_ref[pl.ds(off, tk), :]
        s = _dot_nt(q, k)
        m_new = jnp.maximum(m, jnp.max(s, axis=-1, keepdims=True))
        alpha = jnp.exp(m - m_new)
        p = jnp.exp(s - m_new)
        l = alpha * l + jnp.sum(p, axis=-1, keepdims=True)
        acc = alpha * acc + _dot(p.astype(BF16), v)
        return m_new, l, acc

    m0 = jnp.full((tq, 1), -jnp.inf, F32)
    l0 = jnp.zeros((tq, 1), F32)
    a0 = jnp.zeros((tq, B_VDIM), F32)
    _, l, acc = lax.fori_loop(0, n_chunks, body, (m0, l0, a0))
    o_ref[...] = (acc / l).astype(BF16)


def _mla_attn(q, k, v, L):
    R = q.shape[1]
    tq = 256
    tk = 256
    return pl.pallas_call(
        functools.partial(_mla_kernel, L=L, R=R, tq=tq, tk=tk),
        grid=(B_HEADS, R // tq),
        in_specs=[
            pl.BlockSpec((None, tq, 256), lambda h, i: (h, i, 0)),
            pl.BlockSpec((None, R, 256), lambda h, i: (h, 0, 0)),
            pl.BlockSpec((None, R, B_VDIM), lambda h, i: (h, 0, 0)),
        ],
        out_specs=pl.BlockSpec((tq, B_VDIM), lambda h, i: (i, h)),
        out_shape=jax.ShapeDtypeStruct((R, B_HEADS * B_VDIM), BF16),
        compiler_params=_cparams(("arbitrary", "arbitrary")),
        name="mla_attn",
    )(q, k, v)


SUB = 16


def _split3(x):
    hi = x.astype(BF16)
    r = x - hi.astype(F32)
    mid = r.astype(BF16)
    lo = (r - mid.astype(F32)).astype(BF16)
    return hi, mid, lo


def _hgrn_kernel(z_ref, v_ref, q_ref, lb_ref, pm_ref, o_ref, st_ref):
    n = pl.program_id(1)

    @pl.when(n == 0)
    def _():
        st_ref[...] = jnp.zeros_like(st_ref)

    pm = pm_ref[...]
    r0 = lax.broadcasted_iota(jnp.int32, (BLK, BLK), 0)
    c0 = lax.broadcasted_iota(jnp.int32, (BLK, BLK), 1)
    lt = (r0 >= c0).astype(BF16)
    ones = jnp.ones((BLK, BLK), BF16)
    colgrp = c0 // SUB
    ng = BLK // SUB
    ii = lax.broadcasted_iota(jnp.int32, (ng, SUB, C_DK), 1)

    for h in range(C_HEADS):
        sl = slice(h * C_DK, (h + 1) * C_DK)
        z = _dot(pm, z_ref[:, sl])
        v = _dot(pm, v_ref[:, sl])
        vb = v.astype(BF16)
        qp = _dot(pm, q_ref[:, sl])
        lb = lb_ref[:, sl]

        ls = jnp.minimum(z, 0.0) - jnp.log1p(jnp.exp(-jnp.abs(z)))
        a1 = jnp.log(lb)
        a2 = jnp.log1p(-lb) + ls
        lf = jnp.maximum(a1, a2) + jnp.log1p(jnp.exp(-jnp.abs(a1 - a2)))
        k = (1.0 - lb) * (1.0 / (1.0 + jnp.exp(z)))
        q = qp * _sigmoid(qp) * (C_DK ** -0.5)

        hi, mid, lo = _split3(lf)
        b = _dot(lt, hi) + _dot(lt, mid) + _dot(lt, lo)

        b3 = b.reshape(ng, SUB, C_DK)
        q3 = q.reshape(ng, SUB, C_DK)
        k3 = k.reshape(ng, SUB, C_DK)
        v3 = v.reshape(ng, SUB, C_DK)

        st = st_ref[h]
        o = _dot_nt((q * jnp.exp(b)).astype(BF16), st.astype(BF16))

        eb = jnp.broadcast_to(b3[:, SUB - 1:SUB, :], (ng, SUB, C_DK)).reshape(BLK, C_DK)
        kt = (k * jnp.exp(eb - b)).astype(BF16)
        att = jnp.zeros((BLK, BLK), F32)
        for c in range(ng - 1):
            ec = b[SUB * c + SUB - 1:SUB * c + SUB, :]
            qh = jnp.where(r0 >= SUB * (c + 1), q * jnp.exp(jnp.minimum(b - ec, 0.0)), 0.0)
            att = jnp.where(colgrp == c, _dot_nt(qh.astype(BF16), kt), att)
        o = o + _dot(att.astype(BF16), vb)

        o3 = jnp.zeros((ng, SUB, C_DK), F32)
        for j in range(SUB):
            kj = k3[:, j:j + 1, :]
            bj = b3[:, j:j + 1, :]
            vj = v3[:, j:j + 1, :]
            x = jnp.where(ii >= j, q3 * kj * jnp.exp(jnp.minimum(b3 - bj, 0.0)), 0.0)
            a = _dot(x.reshape(BLK, C_DK).astype(BF16), ones)
            o3 = o3 + a.reshape(ng, SUB, C_DK) * vj
        o = o + o3.reshape(BLK, C_DK)

        bl = b[BLK - 1:BLK, :]
        kl = (k * jnp.exp(bl - b)).astype(BF16)
        st_ref[h] = st * jnp.exp(bl) + _dot(v.T.astype(BF16), kl)

        ohi = o.astype(BF16)
        olo = (o - ohi.astype(F32)).astype(BF16)
        o_ref[:, sl] = _dot(pm, ohi) + _dot(pm, olo)


def _hgrn(p, lb, pmats, li, L):
    R = p.shape[0]
    nb = R // BLK
    nc = L // BLK

    def rb(d, n):
        back = jnp.where(n < nc, nc - 1 - n, nb - 1 - (n - nc))
        return jnp.where(d == 0, n, back)

    return pl.pallas_call(
        _hgrn_kernel,
        grid=(2, nb),
        in_specs=[
            pl.BlockSpec((BLK, 512), lambda d, n: (rb(d, n), P_CFF // 512 + d)),
            pl.BlockSpec((BLK, 512), lambda d, n: (rb(d, n), P_CI // 512)),
            pl.BlockSpec((BLK, 512), lambda d, n: (rb(d, n), P_HQ // 512)),
            pl.BlockSpec((None, None, 1, 512), lambda d, n: (li, d, 0, 0)),
            pl.BlockSpec((None, BLK, BLK), lambda d, n: (d, 0, 0)),
        ],
        out_specs=pl.BlockSpec((None, BLK, 512), lambda d, n: (d, rb(d, n), 0)),
        out_shape=jax.ShapeDtypeStruct((2, R, 512), F32),
        scratch_shapes=[pltpu.VMEM((C_HEADS, C_DK, C_DK), F32)],
        compiler_params=_cparams(("arbitrary", "arbitrary")),
        name="hgrn",
    )(p, p, p, lb, pmats)


def _pool_kernel(up_ref, us_ref, un_ref, w_ref, sc_ref, o_ref, *, L, R):
    i = pl.program_id(0)
    u3 = jnp.concatenate([up_ref[...], us_ref[...], un_ref[...]], axis=0)
    us = us_ref[...].astype(F32)
    r = lax.broadcasted_iota(jnp.int32, (BLK, 3 * BLK), 0) + i * BLK
    s = lax.broadcasted_iota(jnp.int32, (BLK, 3 * BLK), 1) + (i - 1) * BLK
    seq_lo = jnp.where(r < L, 0, L)
    seq_hi = jnp.where(r < L, L, R)
    r1 = lax.broadcasted_iota(jnp.int32, (BLK, LANES), 0) + i * BLK
    seq_lo1 = jnp.where(r1 < L, 0, L)
    seq_hi1 = jnp.where(r1 < L, L, R)
    for g, w in enumerate(POOL_WINDOWS):
        sl = slice(g * LANES, (g + 1) * LANES)
        lo = jnp.maximum(r - w // 2, seq_lo)
        hi = jnp.minimum(r + w - w // 2, seq_hi)
        band = ((s >= lo) & (s < hi)).astype(BF16)
        cnt = (jnp.minimum(r1 + w - w // 2, seq_hi1) - jnp.maximum(r1 - w // 2, seq_lo1)).astype(F32)
        mean = _dot(band, u3[:, sl]) / cnt
        d = (mean - us[:, sl]).astype(BF16)
        o_ref[:, sl] = (_dot(d, w_ref[g]) * sc_ref[:, sl]).astype(BF16)


def _pool(p, w_pool, pool_scale, li, L):
    R = p.shape[0]
    nb = R // BLK
    cb = P_DX // 512
    return pl.pallas_call(
        functools.partial(_pool_kernel, L=L, R=R),
        grid=(nb,),
        in_specs=[
            pl.BlockSpec((BLK, 512), lambda i: (jnp.maximum(i - 1, 0), cb)),
            pl.BlockSpec((BLK, 512), lambda i: (i, cb)),
            pl.BlockSpec((BLK, 512), lambda i: (jnp.minimum(i + 1, nb - 1), cb)),
            pl.BlockSpec((None, len(POOL_WINDOWS), LANES, LANES), lambda i: (li, 0, 0, 0)),
            pl.BlockSpec((None, 1, 512), lambda i: (li, 0, 0)),
        ],
        out_specs=pl.BlockSpec((BLK, 512), lambda i: (i, 0)),
        out_shape=jax.ShapeDtypeStruct((R, 512), BF16),
        compiler_params=_cparams(("arbitrary",)),
        name="pool",
    )(p, p, p, w_pool, pool_scale)


def _merge_kernel(oa_ref, ob_ref, oh_ref, od_ref, gt_ref, mg0_ref, mg1_ref, mg2_ref, mg3_ref,
                  wbr_ref, wout_ref, hn_ref, x_ref, mod_ref, gpost_ref, gnext_ref, modn_ref,
                  xo_ref, ho_ref, *, ctx_tiles):
    is_ctx = pl.program_id(0) < ctx_tiles
    hs = oh_ref[0] + oh_ref[1]
    hn = hn_ref[...]
    oh = jnp.concatenate(
        [_rms(hs[:, h * C_DK:(h + 1) * C_DK], hn[:, h * C_DK:(h + 1) * C_DK]) for h in range(C_HEADS)], axis=1)
    outs = (oa_ref[...].astype(F32), ob_ref[...].astype(F32), oh, od_ref[...].astype(F32))
    mgs = (mg0_ref, mg1_ref, mg2_ref, mg3_ref)
    merged = None
    for nbr in range(N_BRANCH):
        gate = gt_ref[:, nbr * BRANCH_W:(nbr + 1) * BRANCH_W].astype(F32)
        ys = (outs[nbr] * (gate * _sigmoid(gate))).astype(BF16)
        yb = _dot(ys, wbr_ref[nbr])
        term = _sigmoid(mgs[nbr][...].astype(F32)) * yb
        merged = term if merged is None else merged + term
    y = _dot(merged.astype(BF16), wout_ref[...])
    _, _, gate_r = _mod_rows(mod_ref, is_ctx)
    xn = x_ref[...] + gate_r * _rms(y, gpost_ref[...])
    xo_ref[...] = xn
    shift_n, scale_n, _ = _mod_rows(modn_ref, is_ctx)
    ho_ref[...] = (_rms(xn, gnext_ref[...]) * (1.0 + scale_n) + shift_n).astype(BF16)


def _merge(oa, ob, oh, od, p, wbr, wout, hnorm, xs, mod, g_post, g_pre, li, L):
    R = xs.shape[0]
    tm = 256
    ln = min(li + 1, DEPTH - 1)
    row = lambda i: (i, 0)
    gcol = P_GATE // D_MODEL
    mcol = P_MERGE // D_MODEL
    return pl.pallas_call(
        functools.partial(_merge_kernel, ctx_tiles=L // tm),
        grid=(R // tm,),
        in_specs=[
            pl.BlockSpec((tm, 512), row),
            pl.BlockSpec((tm, 512), row),
            pl.BlockSpec((2, tm, 512), lambda i: (0, i, 0)),
            pl.BlockSpec((tm, 512), row),
            pl.BlockSpec((tm, D_MODEL), lambda i: (i, gcol)),
            pl.BlockSpec((tm, D_MODEL), lambda i: (i, mcol)),
            pl.BlockSpec((tm, D_MODEL), lambda i: (i, mcol + 1)),
            pl.BlockSpec((tm, D_MODEL), lambda i: (i, mcol + 2)),
            pl.BlockSpec((tm, D_MODEL), lambda i: (i, mcol + 3)),
            pl.BlockSpec((None, N_BRANCH, BRANCH_W, D_MODEL), lambda i: (li, 0, 0, 0)),
            pl.BlockSpec((None, D_MODEL, D_MODEL), lambda i: (li, 0, 0)),
            pl.BlockSpec((None, 1, 512), lambda i: (li, 0, 0)),
            pl.BlockSpec((tm, D_MODEL), row),
            pl.BlockSpec((None, 8, 3 * D_MODEL), lambda i: (li, 0, 0)),
            pl.BlockSpec((None, 1, D_MODEL), lambda i: (li, 0, 0)),
            pl.BlockSpec((None, 1, D_MODEL), lambda i: (ln, 0, 0)),
            pl.BlockSpec((None, 8, 3 * D_MODEL), lambda i: (ln, 0, 0)),
        ],
        out_specs=[pl.BlockSpec((tm, D_MODEL), row), pl.BlockSpec((tm, D_MODEL), row)],
        out_shape=[jax.ShapeDtypeStruct((R, D_MODEL), F32), jax.ShapeDtypeStruct((R, D_MODEL), BF16)],
        compiler_params=_cparams(("arbitrary",)),
        name="merge",
    )(oa, ob, oh, od, p, p, p, p, p, wbr, wout, hnorm, xs, mod, g_post, g_pre, mod)


def _rope_tables(T, L):
    rows = jnp.repeat(jnp.arange(T // GRID_W, dtype=jnp.int32), GRID_W)
    cols = jnp.tile(jnp.arange(GRID_W, dtype=jnp.int32), T // GRID_W)
    half = 16
    freqs = ROPE_BASE ** (-jnp.arange(half, dtype=F32) / half)
    ar = rows.astype(F32)[:, None] * freqs[None, :]
    ac = cols.astype(F32)[:, None] * freqs[None, :]
    cos = jnp.concatenate([jnp.cos(ar), jnp.cos(ar), jnp.cos(ac), jnp.cos(ac)], axis=1)
    sin = jnp.concatenate([-jnp.sin(ar), jnp.sin(ar), -jnp.sin(ac), jnp.sin(ac)], axis=1)
    cos = jnp.concatenate([jnp.ones((L, 64), F32), cos], axis=0)
    sin = jnp.concatenate([jnp.zeros((L, 64), F32), sin], axis=0)
    return jnp.tile(cos, (1, 2)), jnp.tile(sin, (1, 2))


def kernel(x, c, ctx, c_ctx, w_ada, b_ada, g_pre, g_post, w_in, a_sink, mla_q_norm, w_uq, mla_kv_norm,
           w_ukv, hgrn_lb, hgrn_norm, w_pool, pool_scale, w_branch, w_out):
    assert x.shape[0] == 1 and ctx.shape[0] == 1
    T = x.shape[1]
    L = ctx.shape[1]
    assert T % 256 == 0 and L % 256 == 0 and T % GRID_W == 0

    z64 = jnp.zeros((DEPTH, D_MODEL, 64), BF16)
    z128 = jnp.zeros((DEPTH, D_MODEL, 128), BF16)
    wb = w_in.astype(BF16)
    w_in_p = jnp.concatenate([wb[:, :, :O_CFF], z64, wb[:, :, O_CFF:O_CQ_END], z128, wb[:, :, O_CQ_END:]], axis=2)
    wuq = w_uq.astype(BF16).reshape(DEPTH, B_Q_LORA, B_HEADS, B_NOPE + B_ROPE)
    wuq = jnp.pad(wuq, ((0, 0), (0, 512 - B_Q_LORA), (0, 0), (0, 256 - B_NOPE - B_ROPE))).reshape(DEPTH, 512, 1024)
    qn = jnp.pad(mla_q_norm.astype(F32), ((0, 0), (0, 512 - B_Q_LORA))).reshape(DEPTH, 1, 512)
    kn = mla_kv_norm.astype(F32).reshape(DEPTH, 1, B_KV_LORA)
    wukv = w_ukv.astype(BF16)
    wpool = w_pool.astype(BF16)
    pscale = pool_scale.astype(F32).reshape(DEPTH, 1, BRANCH_W)
    wbr = w_branch.astype(BF16)
    wout = w_out.astype(BF16)
    hnorm = hgrn_norm.astype(F32).reshape(DEPTH, 1, C_HEADS * C_DK)
    gpre = g_pre.astype(F32).reshape(DEPTH, 1, D_MODEL)
    gpost = g_post.astype(F32).reshape(DEPTH, 1, D_MODEL)

    cos, sin = _rope_tables(T, L)
    eye = jnp.eye(BLK, dtype=BF16)
    pmats = jnp.stack([eye, eye[::-1]], axis=0)

    cc = jnp.concatenate([c.astype(F32), c_ctx.astype(F32)[None, :], jnp.zeros((6, D_MODEL), F32)], axis=0)
    mod = _ada_all(cc, w_ada, b_ada)
    lb = _lower_bounds(hgrn_lb)

    xs = jnp.concatenate([ctx[0], x[0]], axis=0).astype(F32)
    h = _prenorm(xs, gpre, mod, 0, L)
    for li in range(DEPTH):
        p = _proj(h, w_in_p, li)
        aq, ak, mq, mk, mv = _prep(p, cos, sin, qn, wuq, kn, wukv, li)
        oa = _window_attn(a_sink[li].astype(F32), aq, ak, p, L)
        ob = _mla_attn(mq, mk, mv, L)
        oh = _hgrn(p, lb, pmats, li, L)
        od = _pool(p, wpool, pscale, li, L)
        xs, h = _merge(oa, ob, oh, od, p, wbr, wout, hnorm, xs, mod, gpost, gpre, li, L)
    return xs[L:][None].astype(x.dtype)
```

```python
import functools

import jax
import jax.numpy as jnp
from jax import lax
from jax.experimental import pallas as pl
from jax.experimental.pallas import tpu as pltpu

F32 = jnp.float32
BF16 = jnp.bfloat16

D_MODEL = 2048
DEPTH = 4
GRID_W = 64
EPS = 1e-6
ROPE_BASE = 10000.0
N_BRANCH = 4
BRANCH_W = 512
A_HEADS = 8
A_HEAD_DIM = 64
WINDOW = 128
B_HEADS = 4
B_NOPE = 128
B_ROPE = 64
B_VDIM = 128
B_Q_LORA = 384
B_KV_LORA = 128
C_HEADS = 4
C_DK = 128
POOL_WINDOWS = (2, 4, 8, 16)

O_CFF = 448
O_CQ_END = 2880
O_COLS = 14144
P_KV = 0
P_CFF = 512
P_CFB = 1024
P_CI = 1536
P_AQ = 2048
P_CQ = 2560
P_HQ = 3072
P_DX = 3584
P_GATE = 4096
P_MERGE = 6144
P_COLS = 14336

BLK = 128
LANES = 128
VMEM_LIMIT = 56 * 1024 * 1024

NT_DIMS = (((1,), (1,)), ((), ()))
LOG2E = 1.4426950408889634


def _cparams(sem):
    return pltpu.CompilerParams(dimension_semantics=sem, vmem_limit_bytes=VMEM_LIMIT)


def _pick_tile(n, cap, mult=BLK):
    best = mult
    t = mult
    while t <= min(n, cap):
        if n % t == 0:
            best = t
        t += mult
    return best


def _dot(a, b):
    return jnp.dot(a, b, preferred_element_type=F32)


def _dot_nt(a, b):
    return lax.dot_general(a, b, NT_DIMS, preferred_element_type=F32)


def _sigmoid(x):
    return 0.5 * jnp.tanh(0.5 * x) + 0.5


def _ada_kernel(cc_ref, w_ref, b_ref, o_ref):
    cc = cc_ref[...]
    s = (cc * _sigmoid(cc)).astype(BF16)
    o_ref[...] = _dot(s, w_ref[...].astype(BF16)) + b_ref[...]


def _ada_all(cc, w_ada, b_ada):
    tn = 1024
    n3 = 3 * D_MODEL
    return pl.pallas_call(
        _ada_kernel,
        grid=(DEPTH, n3 // tn),
        in_specs=[
            pl.BlockSpec((8, D_MODEL), lambda l, j: (0, 0)),
            pl.BlockSpec((None, D_MODEL, tn), lambda l, j: (l, 0, j)),
            pl.BlockSpec((None, 1, tn), lambda l, j: (l, 0, j)),
        ],
        out_specs=pl.BlockSpec((None, 8, tn), lambda l, j: (l, 0, j)),
        out_shape=jax.ShapeDtypeStruct((DEPTH, 8, n3), F32),
        compiler_params=_cparams(("arbitrary", "arbitrary")),
        name="adaln",
    )(cc, w_ada, b_ada.reshape(DEPTH, 1, n3))


def _lb_kernel(x_ref, o_ref):
    x = x_ref[...]
    m = jnp.max(x, axis=0, keepdims=True)
    e = jnp.exp(x - m)
    sm = e / jnp.sum(e, axis=0, keepdims=True)
    rows = []
    run = sm[0:1]
    first = run
    for l in range(DEPTH):
        if l > 0:
            run = run + sm[l:l + 1]
        rows.append(run - first)
    o_ref[...] = jnp.concatenate(rows, axis=0)


def _lower_bounds(hgrn_lb):
    n = 2 * C_HEADS * C_DK
    out = pl.pallas_call(
        _lb_kernel,
        out_shape=jax.ShapeDtypeStruct((DEPTH, n), F32),
        name="hgrn_lb",
    )(hgrn_lb.reshape(DEPTH, n).astype(F32))
    return out.reshape(DEPTH, 2, 1, C_HEADS * C_DK)


def _rms(x, g):
    return x * lax.rsqrt(jnp.mean(x * x, axis=-1, keepdims=True) + EPS) * g


def _mod_rows(mod_ref, is_ctx):
    m = mod_ref[...]
    sel = jnp.where(is_ctx, m[1:2, :], m[0:1, :])
    return sel[:, 0:D_MODEL], sel[:, D_MODEL:2 * D_MODEL], sel[:, 2 * D_MODEL:3 * D_MODEL]


def _prenorm_kernel(x_ref, g_ref, mod_ref, h_ref, *, ctx_tiles):
    is_ctx = pl.program_id(0) < ctx_tiles
    shift, scale, _ = _mod_rows(mod_ref, is_ctx)
    h = _rms(x_ref[...], g_ref[...]) * (1.0 + scale) + shift
    h_ref[...] = h.astype(BF16)


def _prenorm(xs, g_pre, mod, li, L):
    R = xs.shape[0]
    tm = 256
    return pl.pallas_call(
        functools.partial(_prenorm_kernel, ctx_tiles=L // tm),
        grid=(R // tm,),
        in_specs=[
            pl.BlockSpec((tm, D_MODEL), lambda i: (i, 0)),
            pl.BlockSpec((None, 1, D_MODEL), lambda i: (li, 0, 0)),
            pl.BlockSpec((None, 8, 3 * D_MODEL), lambda i: (li, 0, 0)),
        ],
        out_specs=pl.BlockSpec((tm, D_MODEL), lambda i: (i, 0)),
        out_shape=jax.ShapeDtypeStruct((R, D_MODEL), BF16),
        compiler_params=_cparams(("arbitrary",)),
        name="prenorm",
    )(xs, g_pre, mod)


def _proj_kernel(h_ref, w_ref, o_ref):
    o_ref[...] = _dot(h_ref[...], w_ref[...]).astype(o_ref.dtype)


def _proj(h, w_in_p, li):
    R = h.shape[0]
    tm = _pick_tile(R, 1408)
    tn = 1024
    return pl.pallas_call(
        _proj_kernel,
        grid=(R // tm, P_COLS // tn),
        in_specs=[
            pl.BlockSpec((tm, D_MODEL), lambda i, j: (i, 0)),
            pl.BlockSpec((None, D_MODEL, tn), lambda i, j: (li, 0, j)),
        ],
        out_specs=pl.BlockSpec((tm, tn), lambda i, j: (i, j)),
        out_shape=jax.ShapeDtypeStruct((R, P_COLS), BF16),
        compiler_params=_cparams(("arbitrary", "arbitrary")),
        name="proj",
    )(h, w_in_p)


def _rope_tile(t, cos, sin):
    lane = lax.broadcasted_iota(jnp.int32, t.shape, 1)
    first = (lane & 16) == 0
    partner = jnp.where(first, pltpu.roll(t, LANES - 16, 1), pltpu.roll(t, 16, 1))
    return t * cos + partner * sin


def _prep_kernel(aq_ref, kv_ref, cq_ref, cos_ref, sin_ref, qn_ref, wuq_ref, kn_ref, wukv_ref,
                 aq_o, ak_o, q_o, k_o, v_o):
    cos = cos_ref[...]
    sin = sin_ref[...]
    a_scale = A_HEAD_DIM ** -0.5
    for t in range(A_HEADS * A_HEAD_DIM // LANES):
        sl = slice(t * LANES, (t + 1) * LANES)
        aq_o[:, sl] = (_rope_tile(aq_ref[:, sl].astype(F32), cos, sin) * a_scale).astype(BF16)
    ak_o[...] = _rope_tile(kv_ref[:, 0:128].astype(F32), cos, sin).astype(BF16)

    ckv = kv_ref[:, 256:384].astype(F32)
    ckv = _rms(ckv, kn_ref[...])
    kv = _dot(ckv.astype(BF16), wukv_ref[...])
    kr = _rope_tile(kv_ref[:, 384:512].astype(F32), cos, sin).astype(BF16)
    for h in range(B_HEADS):
        k_o[h, :, 0:128] = kv[:, 256 * h:256 * h + 128].astype(BF16)
        k_o[h, :, 128:256] = kr
        v_o[h, :, 0:128] = kv[:, 256 * h + 128:256 * h + 256].astype(BF16)
        v_o[h, :, 128:256] = jnp.ones((kv.shape[0], 128), BF16)

    cq = cq_ref[...].astype(F32)
    cq = cq * lax.rsqrt(jnp.sum(cq * cq, axis=-1, keepdims=True) * (1.0 / B_Q_LORA) + EPS) * qn_ref[...]
    q = _dot(cq.astype(BF16), wuq_ref[...])
    b_scale = (B_NOPE + B_ROPE) ** -0.5 * LOG2E
    for h in range(B_HEADS):
        q_o[h, :, 0:128] = (q[:, 256 * h:256 * h + 128] * b_scale).astype(BF16)
        q_o[h, :, 128:256] = (_rope_tile(q[:, 256 * h + 128:256 * h + 256], cos, sin) * b_scale).astype(BF16)


def _prep(p, cos, sin, qn, wuq, kn, wukv, li):
    R = p.shape[0]
    tm = 256
    row = lambda i: (i, 0)
    return pl.pallas_call(
        _prep_kernel,
        grid=(R // tm,),
        in_specs=[
            pl.BlockSpec((tm, 512), lambda i: (i, P_AQ // 512)),
            pl.BlockSpec((tm, 512), lambda i: (i, P_KV // 512)),
            pl.BlockSpec((tm, 512), lambda i: (i, P_CQ // 512)),
            pl.BlockSpec((tm, LANES), row),
            pl.BlockSpec((tm, LANES), row),
            pl.BlockSpec((None, 1, 512), lambda i: (li, 0, 0)),
            pl.BlockSpec((None, 512, 1024), lambda i: (li, 0, 0)),
            pl.BlockSpec((None, 1, B_KV_LORA), lambda i: (li, 0, 0)),
            pl.BlockSpec((None, B_KV_LORA, 1024), lambda i: (li, 0, 0)),
        ],
        out_specs=[
            pl.BlockSpec((tm, 512), row),
            pl.BlockSpec((tm, LANES), row),
            pl.BlockSpec((B_HEADS, tm, 256), lambda i: (0, i, 0)),
            pl.BlockSpec((B_HEADS, tm, 256), lambda i: (0, i, 0)),
            pl.BlockSpec((B_HEADS, tm, 256), lambda i: (0, i, 0)),
        ],
        out_shape=[
            jax.ShapeDtypeStruct((R, 512), BF16),
            jax.ShapeDtypeStruct((R, LANES), BF16),
            jax.ShapeDtypeStruct((B_HEADS, R, 256), BF16),
            jax.ShapeDtypeStruct((B_HEADS, R, 256), BF16),
            jax.ShapeDtypeStruct((B_HEADS, R, 256), BF16),
        ],
        compiler_params=_cparams(("arbitrary",)),
        name="prep",
    )(p, p, p, cos, sin, qn, wuq, kn, wukv)


def _swap_halves(t):
    return jnp.concatenate([t[:, 64:], t[:, :64]], axis=1)


def _win_kernel(sink_ref, q_ref, kp_ref, ks_ref, kn_ref, vp_ref, vs_ref, vn_ref, kc_ref, vc_ref, o_ref,
                *, L, R):
    i = pl.program_id(0)
    q = q_ref[...]
    kl = jnp.concatenate([kp_ref[...], ks_ref[...], kn_ref[...]], axis=0)
    vl = jnp.concatenate([vp_ref[...], vs_ref[...], vn_ref[...]], axis=0)
    kc = kc_ref[...]
    vc = vc_ref[...]
    lo = lax.broadcasted_iota(jnp.int32, (BLK, LANES), 1) < 64
    zero = jnp.zeros((BLK, LANES), BF16)

    rows = 4 * BLK
    rr = lax.broadcasted_iota(jnp.int32, (rows, 3 * BLK), 0)
    cc = lax.broadcasted_iota(jnp.int32, (rows, 3 * BLK), 1)
    qpos = (rr & (BLK - 1)) + i * BLK
    kpos = cc + (i - 1) * BLK
    ok = ((kpos >= L) & (kpos < R) & (qpos >= L)
          & (kpos - qpos <= WINDOW) & (qpos - kpos <= WINDOW))
    hrow = lax.broadcasted_iota(jnp.int32, (rows, 1), 0) // BLK

    for g in range(2):
        t0 = q[:, 256 * g:256 * g + 128]
        t1 = q[:, 256 * g + 128:256 * g + 256]
        if g == 0:
            parts = [jnp.where(lo, t0, zero), jnp.where(lo, _swap_halves(t0), zero),
                     jnp.where(lo, t1, zero), jnp.where(lo, _swap_halves(t1), zero)]
        else:
            parts = [jnp.where(lo, zero, _swap_halves(t0)), jnp.where(lo, zero, t0),
                     jnp.where(lo, zero, _swap_halves(t1)), jnp.where(lo, zero, t1)]
        qs = jnp.concatenate(parts, axis=0)
        s_l = jnp.where(ok, _dot_nt(qs, kl), -jnp.inf)
        s_c = _dot_nt(qs, kc)
        sink = jnp.zeros((rows, 1), F32)
        for j in range(4):
            sink = jnp.where(hrow == j, sink_ref[4 * g + j], sink)
        m = jnp.maximum(jnp.maximum(jnp.max(s_l, axis=-1, keepdims=True),
                                    jnp.max(s_c, axis=-1, keepdims=True)), sink)
        p_l = jnp.exp(s_l - m)
        p_c = jnp.exp(s_c - m)
        den = (jnp.sum(p_l, axis=-1, keepdims=True) + jnp.sum(p_c, axis=-1, keepdims=True)
               + jnp.exp(sink - m))
        o = (_dot(p_l.astype(BF16), vl) + _dot(p_c.astype(BF16), vc)) / den
        for pair in range(2):
            a = o[2 * pair * BLK:(2 * pair + 1) * BLK]
            b = o[(2 * pair + 1) * BLK:(2 * pair + 2) * BLK]
            if g == 0:
                tile = jnp.where(lo, a, _swap_halves(b))
            else:
                tile = jnp.where(lo, _swap_halves(a), b)
            c0 = 256 * g + 128 * pair
            o_ref[:, c0:c0 + 128] = tile.astype(BF16)


def _window_attn(sink, aq, ak, p, L):
    R = aq.shape[0]
    nb = R // BLK
    prev = lambda c: (lambda i: (jnp.maximum(i - 1, 0), c))
    this = lambda c: (lambda i: (i, c))
    nxt = lambda c: (lambda i: (jnp.minimum(i + 1, nb - 1), c))
    return pl.pallas_call(
        functools.partial(_win_kernel, L=L, R=R),
        grid=(nb,),
        in_specs=[
            pl.BlockSpec(memory_space=pltpu.SMEM),
            pl.BlockSpec((BLK, 512), lambda i: (i, 0)),
            pl.BlockSpec((BLK, LANES), prev(0)),
            pl.BlockSpec((BLK, LANES), this(0)),
            pl.BlockSpec((BLK, LANES), nxt(0)),
            pl.BlockSpec((BLK, LANES), prev(1)),
            pl.BlockSpec((BLK, LANES), this(1)),
            pl.BlockSpec((BLK, LANES), nxt(1)),
            pl.BlockSpec((L, LANES), lambda i: (0, 0)),
            pl.BlockSpec((L, LANES), lambda i: (0, 1)),
        ],
        out_specs=pl.BlockSpec((BLK, 512), lambda i: (i, 0)),
        out_shape=jax.ShapeDtypeStruct((R, 512), BF16),
        compiler_params=_cparams(("arbitrary",)),
        name="window_attn",
    )(sink, aq, ak, ak, ak, p, p, p, ak, p)


def _softmax_out(acc):
    return (acc[:, 0:B_VDIM] / acc[:, B_VDIM:2 * B_VDIM]).astype(BF16)


def _mla_kernel(qn_ref, q0_ref, k_ref, v_ref, o_ref, s0_ref, s1_ref, m0_ref, m1_ref, *, L, tq, tk):
    i = pl.program_id(1)
    nk = k_ref.shape[0] // tk

    @pl.when(i == 0)
    def _():
        s1_ref[...] = jnp.zeros_like(s1_ref)
        m1_ref[...] = jnp.zeros_like(m1_ref)

    def step(sw_ref, mw_ref, sr_ref, mr_ref):
        q = qn_ref[...]
        m_prev = mr_ref[...]
        mrun = jnp.full((tq, LANES), -jnp.inf, F32)
        acc = jnp.zeros((tq, 2 * B_VDIM), F32)
        for c in range(nk):
            rows = slice(c * tk, (c + 1) * tk)
            s = _dot_nt(q, k_ref[rows, :])
            sw_ref[c] = s
            for t in range(tk // LANES):
                mrun = jnp.maximum(mrun, s[:, t * LANES:(t + 1) * LANES])
            sp = sr_ref[c]
            p = jnp.concatenate(
                [jnp.exp2(sp[:, t * LANES:(t + 1) * LANES] - m_prev) for t in range(tk // LANES)], axis=1)
            acc = acc + _dot(p.astype(BF16), v_ref[rows, :])
        mw_ref[...] = jnp.broadcast_to(jnp.max(mrun, axis=-1, keepdims=True), (tq, LANES))
        o_ref[...] = _softmax_out(acc)

    @pl.when(i % 2 == 0)
    def _():
        step(s0_ref, m0_ref, s1_ref, m1_ref)

    @pl.when(i % 2 == 1)
    def _():
        step(s1_ref, m1_ref, s0_ref, m0_ref)

    @pl.when(i == 0)
    def _():
        s = _dot_nt(q0_ref[...], k_ref[0:L, :])
        p = jnp.exp2(s - jnp.max(s, axis=-1, keepdims=True))
        o_ref[...] = _softmax_out(_dot(p.astype(BF16), v_ref[0:L, :]))


def _mla_attn(q, k, v, L):
    R = q.shape[1]
    tq = 256
    tk = 256
    assert L == tq
    nt = R // tq
    return pl.pallas_call(
        functools.partial(_mla_kernel, L=L, tq=tq, tk=tk),
        grid=(B_HEADS, nt),
        in_specs=[
            pl.BlockSpec((None, tq, 256), lambda h, i: (h, jnp.minimum(i + 1, nt - 1), 0)),
            pl.BlockSpec((None, tq, 256), lambda h, i: (h, 0, 0)),
            pl.BlockSpec((None, R, 256), lambda h, i: (h, 0, 0)),
            pl.BlockSpec((None, R, 256), lambda h, i: (h, 0, 0)),
        ],
        out_specs=pl.BlockSpec((tq, B_VDIM), lambda h, i: (i, h)),
        out_shape=jax.ShapeDtypeStruct((R, B_HEADS * B_VDIM), BF16),
        scratch_shapes=[
            pltpu.VMEM((R // tk, tq, tk), F32),
            pltpu.VMEM((R // tk, tq, tk), F32),
            pltpu.VMEM((tq, LANES), F32),
            pltpu.VMEM((tq, LANES), F32),
        ],
        compiler_params=_cparams(("arbitrary", "arbitrary")),
        name="mla_attn",
    )(q, q, k, v)


LEVELS = (64, 32, 16, 8, 4, 2, 1)


def _split3(x):
    hi = x.astype(BF16)
    r = x - hi.astype(F32)
    mid = r.astype(BF16)
    lo = (r - mid.astype(F32)).astype(BF16)
    return hi, mid, lo


def _hgrn_kernel(z_ref, v_ref, q_ref, lb_ref, pm_ref, o_ref, st_ref):
    n = pl.program_id(1)

    @pl.when(n == 0)
    def _():
        st_ref[...] = jnp.zeros_like(st_ref)

    pm = pm_ref[...]
    width = C_HEADS * C_DK
    heads = [slice(h * C_DK, (h + 1) * C_DK) for h in range(C_HEADS)]
    r0 = lax.broadcasted_iota(jnp.int32, (BLK, BLK), 0)
    c0 = lax.broadcasted_iota(jnp.int32, (BLK, BLK), 1)
    lt = (r0 >= c0).astype(BF16)
    ones = jnp.ones((BLK, BLK), BF16)
    rw = lax.broadcasted_iota(jnp.int32, (BLK, width), 0)
    sub8 = lax.broadcasted_iota(jnp.int32, (BLK // 8, 8, width), 1)

    z = _dot(pm, z_ref[...])
    v = _dot(pm, v_ref[...])
    vb = v.astype(BF16)
    qp = _dot(pm, q_ref[...])
    lb = lb_ref[...]

    ls = jnp.minimum(z, 0.0) - jnp.log(1.0 + jnp.exp(-jnp.abs(z)))
    a1 = jnp.log(lb)
    a2 = jnp.log1p(-lb) + ls
    lf = jnp.maximum(a1, a2) + jnp.log(1.0 + jnp.exp(-jnp.abs(a1 - a2)))
    k = (1.0 - lb) * (1.0 / (1.0 + jnp.exp(z)))
    q = qp * _sigmoid(qp) * (C_DK ** -0.5)

    hi, mid, lo = _split3(lf * LOG2E)
    b = _dot(lt, hi) + _dot(lt, mid) + _dot(lt, lo)

    sts = [st_ref[h] for h in range(C_HEADS)]
    qe = (q * jnp.exp2(b)).astype(BF16)
    outs = [_dot_nt(qe[:, sl], st.astype(BF16)) for sl, st in zip(heads, sts)]

    b8 = b.reshape(BLK // 8, 8, width)
    atts = [jnp.zeros((BLK, BLK), F32) for _ in heads]
    for s in LEVELS:
        if s >= 8:
            g = BLK // (2 * s)
            e = jnp.broadcast_to(b.reshape(g, 2 * s, width)[:, s - 1:s, :], (g, 2 * s, width))
        elif s == 4:
            e = jnp.broadcast_to(b8[:, 3:4, :], b8.shape)
        elif s == 2:
            e = jnp.where(sub8 < 4, jnp.broadcast_to(b8[:, 1:2, :], b8.shape),
                          jnp.broadcast_to(b8[:, 5:6, :], b8.shape))
        else:
            e = jnp.where((sub8 & 1) == 0, b8, pltpu.roll(b8, 1, 1))
        w = jnp.exp2(-jnp.abs(b - e.reshape(BLK, width)))
        right = (rw & s) != 0
        qt = jnp.where(right, q * w, 0.0).astype(BF16)
        kt = jnp.where(right, 0.0, k * w).astype(BF16)
        pair = (r0 // (2 * s)) == (c0 // (2 * s))
        atts = [jnp.where(pair, _dot_nt(qt[:, sl], kt[:, sl]), att) for sl, att in zip(heads, atts)]
    dg = (q * k).astype(BF16)
    outs = [o + _dot(att.astype(BF16), vb[:, sl]) + _dot(dg[:, sl], ones) * v[:, sl]
            for sl, o, att in zip(heads, outs, atts)]

    bl = b[BLK - 1:BLK, :]
    kl = (k * jnp.exp2(bl - b)).astype(BF16)
    dec = jnp.exp2(bl)
    for h, sl in enumerate(heads):
        st_ref[h] = sts[h] * dec[:, sl] + _dot(v[:, sl].T.astype(BF16), kl[:, sl])

    o = jnp.concatenate(outs, axis=1)
    ohi = o.astype(BF16)
    olo = (o - ohi.astype(F32)).astype(BF16)
    o_ref[...] = _dot(pm, ohi) + _dot(pm, olo)


def _hgrn(p, lb, pmats, li, L):
    R = p.shape[0]
    nb = R // BLK
    nc = L // BLK

    def rb(d, n):
        back = jnp.where(n < nc, nc - 1 - n, nb - 1 - (n - nc))
        return jnp.where(d == 0, n, back)

    return pl.pallas_call(
        _hgrn_kernel,
        grid=(2, nb),
        in_specs=[
            pl.BlockSpec((BLK, 512), lambda d, n: (rb(d, n), P_CFF // 512 + d)),
            pl.BlockSpec((BLK, 512), lambda d, n: (rb(d, n), P_CI // 512)),
            pl.BlockSpec((BLK, 512), lambda d, n: (rb(d, n), P_HQ // 512)),
            pl.BlockSpec((None, None, 1, 512), lambda d, n: (li, d, 0, 0)),
            pl.BlockSpec((None, BLK, BLK), lambda d, n: (d, 0, 0)),
        ],
        out_specs=pl.BlockSpec((None, BLK, 512), lambda d, n: (d, rb(d, n), 0)),
        out_shape=jax.ShapeDtypeStruct((2, R, 512), F32),
        scratch_shapes=[pltpu.VMEM((C_HEADS, C_DK, C_DK), F32)],
        compiler_params=_cparams(("arbitrary", "arbitrary")),
        name="hgrn",
    )(p, p, p, lb, pmats)


def _pool_kernel(up_ref, us_ref, un_ref, w_ref, sc_ref, o_ref, *, L, R):
    i = pl.program_id(0)
    u3 = jnp.concatenate([up_ref[...], us_ref[...], un_ref[...]], axis=0)
    us = us_ref[...].astype(F32)
    r = lax.broadcasted_iota(jnp.int32, (BLK, 3 * BLK), 0) + i * BLK
    s = lax.broadcasted_iota(jnp.int32, (BLK, 3 * BLK), 1) + (i - 1) * BLK
    seq_lo = jnp.where(r < L, 0, L)
    seq_hi = jnp.where(r < L, L, R)
    r1 = lax.broadcasted_iota(jnp.int32, (BLK, LANES), 0) + i * BLK
    seq_lo1 = jnp.where(r1 < L, 0, L)
    seq_hi1 = jnp.where(r1 < L, L, R)
    for g, w in enumerate(POOL_WINDOWS):
        sl = slice(g * LANES, (g + 1) * LANES)
        lo = jnp.maximum(r - w // 2, seq_lo)
        hi = jnp.minimum(r + w - w // 2, seq_hi)
        band = ((s >= lo) & (s < hi)).astype(BF16)
        cnt = (jnp.minimum(r1 + w - w // 2, seq_hi1) - jnp.maximum(r1 - w // 2, seq_lo1)).astype(F32)
        mean = _dot(band, u3[:, sl]) / cnt
        d = (mean - us[:, sl]).astype(BF16)
        o_ref[:, sl] = (_dot(d, w_ref[g]) * sc_ref[:, sl]).astype(BF16)


def _pool(p, w_pool, pool_scale, li, L):
    R = p.shape[0]
    nb = R // BLK
    cb = P_DX // 512
    return pl.pallas_call(
        functools.partial(_pool_kernel, L=L, R=R),
        grid=(nb,),
        in_specs=[
            pl.BlockSpec((BLK, 512), lambda i: (jnp.maximum(i - 1, 0), cb)),
            pl.BlockSpec((BLK, 512), lambda i: (i, cb)),
            pl.BlockSpec((BLK, 512), lambda i: (jnp.minimum(i + 1, nb - 1), cb)),
            pl.BlockSpec((None, len(POOL_WINDOWS), LANES, LANES), lambda i: (li, 0, 0, 0)),
            pl.BlockSpec((None, 1, 512), lambda i: (li, 0, 0)),
        ],
        out_specs=pl.BlockSpec((BLK, 512), lambda i: (i, 0)),
        out_shape=jax.ShapeDtypeStruct((R, 512), BF16),
        compiler_params=_cparams(("arbitrary",)),
        name="pool",
    )(p, p, p, w_pool, pool_scale)


def _merge_kernel(oa_ref, ob_ref, oh_ref, od_ref, gt_ref, mg0_ref, mg1_ref, mg2_ref, mg3_ref,
                  wbr_ref, wout_ref, hn_ref, x_ref, mod_ref, gpost_ref, gnext_ref, modn_ref,
                  xo_ref, ho_ref, *, ctx_tiles):
    is_ctx = pl.program_id(0) < ctx_tiles
    hs = oh_ref[0] + oh_ref[1]
    hn = hn_ref[...]
    oh = jnp.concatenate(
        [_rms(hs[:, h * C_DK:(h + 1) * C_DK], hn[:, h * C_DK:(h + 1) * C_DK]) for h in range(C_HEADS)], axis=1)
    outs = (oa_ref[...].astype(F32), ob_ref[...].astype(F32), oh, od_ref[...].astype(F32))
    mgs = (mg0_ref, mg1_ref, mg2_ref, mg3_ref)
    merged = None
    for nbr in range(N_BRANCH):
        gate = gt_ref[:, nbr * BRANCH_W:(nbr + 1) * BRANCH_W].astype(F32)
        ys = (outs[nbr] * (gate * _sigmoid(gate))).astype(BF16)
        yb = _dot(ys, wbr_ref[nbr])
        term = _sigmoid(mgs[nbr][...].astype(F32)) * yb
        merged = term if merged is None else merged + term
    y = _dot(merged.astype(BF16), wout_ref[...])
    _, _, gate_r = _mod_rows(mod_ref, is_ctx)
    xn = x_ref[...] + gate_r * _rms(y, gpost_ref[...])
    xo_ref[...] = xn
    shift_n, scale_n, _ = _mod_rows(modn_ref, is_ctx)
    ho_ref[...] = (_rms(xn, gnext_ref[...]) * (1.0 + scale_n) + shift_n).astype(BF16)


def _merge(oa, ob, oh, od, p, wbr, wout, hnorm, xs, mod, g_post, g_pre, li, L):
    R = xs.shape[0]
    tm = 256
    ln = min(li + 1, DEPTH - 1)
    row = lambda i: (i, 0)
    gcol = P_GATE // D_MODEL
    mcol = P_MERGE // D_MODEL
    return pl.pallas_call(
        functools.partial(_merge_kernel, ctx_tiles=L // tm),
        grid=(R // tm,),
        in_specs=[
            pl.BlockSpec((tm, 512), row),
            pl.BlockSpec((tm, 512), row),
            pl.BlockSpec((2, tm, 512), lambda i: (0, i, 0)),
            pl.BlockSpec((tm, 512), row),
            pl.BlockSpec((tm, D_MODEL), lambda i: (i, gcol)),
            pl.BlockSpec((tm, D_MODEL), lambda i: (i, mcol)),
            pl.BlockSpec((tm, D_MODEL), lambda i: (i, mcol + 1)),
            pl.BlockSpec((tm, D_MODEL), lambda i: (i, mcol + 2)),
            pl.BlockSpec((tm, D_MODEL), lambda i: (i, mcol + 3)),
            pl.BlockSpec((None, N_BRANCH, BRANCH_W, D_MODEL), lambda i: (li, 0, 0, 0)),
            pl.BlockSpec((None, D_MODEL, D_MODEL), lambda i: (li, 0, 0)),
            pl.BlockSpec((None, 1, 512), lambda i: (li, 0, 0)),
            pl.BlockSpec((tm, D_MODEL), row),
            pl.BlockSpec((None, 8, 3 * D_MODEL), lambda i: (li, 0, 0)),
            pl.BlockSpec((None, 1, D_MODEL), lambda i: (li, 0, 0)),
            pl.BlockSpec((None, 1, D_MODEL), lambda i: (ln, 0, 0)),
            pl.BlockSpec((None, 8, 3 * D_MODEL), lambda i: (ln, 0, 0)),
        ],
        out_specs=[pl.BlockSpec((tm, D_MODEL), row), pl.BlockSpec((tm, D_MODEL), row)],
        out_shape=[jax.ShapeDtypeStruct((R, D_MODEL), F32), jax.ShapeDtypeStruct((R, D_MODEL), BF16)],
        compiler_params=_cparams(("arbitrary",)),
        name="merge",
    )(oa, ob, oh, od, p, p, p, p, p, wbr, wout, hnorm, xs, mod, g_post, g_pre, mod)


def _rope_tables(T, L):
    rows = jnp.repeat(jnp.arange(T // GRID_W, dtype=jnp.int32), GRID_W)
    cols = jnp.tile(jnp.arange(GRID_W, dtype=jnp.int32), T // GRID_W)
    half = 16
    freqs = ROPE_BASE ** (-jnp.arange(half, dtype=F32) / half)
    ar = rows.astype(F32)[:, None] * freqs[None, :]
    ac = cols.astype(F32)[:, None] * freqs[None, :]
    cos = jnp.concatenate([jnp.cos(ar), jnp.cos(ar), jnp.cos(ac), jnp.cos(ac)], axis=1)
    sin = jnp.concatenate([-jnp.sin(ar), jnp.sin(ar), -jnp.sin(ac), jnp.sin(ac)], axis=1)
    cos = jnp.concatenate([jnp.ones((L, 64), F32), cos], axis=0)
    sin = jnp.concatenate([jnp.zeros((L, 64), F32), sin], axis=0)
    return jnp.tile(cos, (1, 2)), jnp.tile(sin, (1, 2))


def kernel(x, c, ctx, c_ctx, w_ada, b_ada, g_pre, g_post, w_in, a_sink, mla_q_norm, w_uq, mla_kv_norm,
           w_ukv, hgrn_lb, hgrn_norm, w_pool, pool_scale, w_branch, w_out):
    assert x.shape[0] == 1 and ctx.shape[0] == 1
    T = x.shape[1]
    L = ctx.shape[1]
    assert T % 256 == 0 and L % 256 == 0 and T % GRID_W == 0

    z64 = jnp.zeros((DEPTH, D_MODEL, 64), BF16)
    z128 = jnp.zeros((DEPTH, D_MODEL, 128), BF16)
    wb = w_in.astype(BF16)
    w_in_p = jnp.concatenate([wb[:, :, :O_CFF], z64, wb[:, :, O_CFF:O_CQ_END], z128, wb[:, :, O_CQ_END:]], axis=2)
    wuq = w_uq.astype(BF16).reshape(DEPTH, B_Q_LORA, B_HEADS, B_NOPE + B_ROPE)
    wuq = jnp.pad(wuq, ((0, 0), (0, 512 - B_Q_LORA), (0, 0), (0, 256 - B_NOPE - B_ROPE))).reshape(DEPTH, 512, 1024)
    qn = jnp.pad(mla_q_norm.astype(F32), ((0, 0), (0, 512 - B_Q_LORA))).reshape(DEPTH, 1, 512)
    kn = mla_kv_norm.astype(F32).reshape(DEPTH, 1, B_KV_LORA)
    wukv = w_ukv.astype(BF16)
    wpool = w_pool.astype(BF16)
    pscale = pool_scale.astype(F32).reshape(DEPTH, 1, BRANCH_W)
    wbr = w_branch.astype(BF16)
    wout = w_out.astype(BF16)
    hnorm = hgrn_norm.astype(F32).reshape(DEPTH, 1, C_HEADS * C_DK)
    gpre = g_pre.astype(F32).reshape(DEPTH, 1, D_MODEL)
    gpost = g_post.astype(F32).reshape(DEPTH, 1, D_MODEL)

    cos, sin = _rope_tables(T, L)
    eye = jnp.eye(BLK, dtype=BF16)
    pmats = jnp.stack([eye, eye[::-1]], axis=0)

    cc = jnp.concatenate([c.astype(F32), c_ctx.astype(F32)[None, :], jnp.zeros((6, D_MODEL), F32)], axis=0)
    mod = _ada_all(cc, w_ada, b_ada)
    lb = _lower_bounds(hgrn_lb)

    xs = jnp.concatenate([ctx[0], x[0]], axis=0).astype(F32)
    h = _prenorm(xs, gpre, mod, 0, L)
    for li in range(DEPTH):
        p = _proj(h, w_in_p, li)
        aq, ak, mq, mk, mv = _prep(p, cos, sin, qn, wuq, kn, wukv, li)
        oa = _window_attn(a_sink[li].astype(F32), aq, ak, p, L)
        ob = _mla_attn(mq, mk, mv, L)
        oh = _hgrn(p, lb, pmats, li, L)
        od = _pool(p, wpool, pscale, li, L)
        xs, h = _merge(oa, ob, oh, od, p, wbr, wout, hnorm, xs, mod, gpost, gpre, li, L)
    return xs[L:][None].astype(x.dtype)
```

```python
import functools

import jax
import jax.numpy as jnp
from jax import lax
from jax.experimental import pallas as pl
from jax.experimental.pallas import tpu as pltpu

F32 = jnp.float32
BF16 = jnp.bfloat16

D_MODEL = 2048
DEPTH = 4
GRID_W = 64
EPS = 1e-6
ROPE_BASE = 10000.0
N_BRANCH = 4
BRANCH_W = 512
A_HEADS = 8
A_HEAD_DIM = 64
WINDOW = 128
B_HEADS = 4
B_NOPE = 128
B_ROPE = 64
B_VDIM = 128
B_Q_LORA = 384
B_KV_LORA = 128
C_HEADS = 4
C_DK = 128
POOL_WINDOWS = (2, 4, 8, 16)

O_P1_END = 512
O_PA = (448, 2880)
O_PC = (2880, 3904)
O_PB = 3904
PA_CFF = 0
PA_CI = 1024
PA_AQ = 1536
PA_CQ = 2048
PC_HQ = 0
PC_DX = 512

BLK = 128
LANES = 128
VMEM_LIMIT = 56 * 1024 * 1024

NT_DIMS = (((1,), (1,)), ((), ()))
LOG2E = 1.4426950408889634


def _cparams(sem):
    return pltpu.CompilerParams(dimension_semantics=sem, vmem_limit_bytes=VMEM_LIMIT)


def _pick_tile(n, cap, mult=BLK):
    best = mult
    t = mult
    while t <= min(n, cap):
        if n % t == 0:
            best = t
        t += mult
    return best


def _dot(a, b):
    return jnp.dot(a, b, preferred_element_type=F32)


def _dot_nt(a, b):
    return lax.dot_general(a, b, NT_DIMS, preferred_element_type=F32)


def _sigmoid(x):
    return 0.5 * jnp.tanh(0.5 * x) + 0.5


def _ada_kernel(cc_ref, w_ref, b_ref, o_ref):
    cc = cc_ref[...]
    s = (cc * _sigmoid(cc)).astype(BF16)
    o_ref[...] = _dot(s, w_ref[...].astype(BF16)) + b_ref[...]


def _ada_all(cc, w_ada, b_ada):
    tn = 1024
    n3 = 3 * D_MODEL
    return pl.pallas_call(
        _ada_kernel,
        grid=(DEPTH, n3 // tn),
        in_specs=[
            pl.BlockSpec((8, D_MODEL), lambda l, j: (0, 0)),
            pl.BlockSpec((None, D_MODEL, tn), lambda l, j: (l, 0, j)),
            pl.BlockSpec((None, 1, tn), lambda l, j: (l, 0, j)),
        ],
        out_specs=pl.BlockSpec((None, 8, tn), lambda l, j: (l, 0, j)),
        out_shape=jax.ShapeDtypeStruct((DEPTH, 8, n3), F32),
        compiler_params=_cparams(("arbitrary", "arbitrary")),
        name="adaln",
    )(cc, w_ada, b_ada.reshape(DEPTH, 1, n3))


def _lb_kernel(x_ref, o_ref):
    x = x_ref[...]
    m = jnp.max(x, axis=0, keepdims=True)
    e = jnp.exp(x - m)
    sm = e / jnp.sum(e, axis=0, keepdims=True)
    rows = []
    run = sm[0:1]
    first = run
    for l in range(DEPTH):
        if l > 0:
            run = run + sm[l:l + 1]
        rows.append(run - first)
    o_ref[...] = jnp.concatenate(rows, axis=0)


def _lower_bounds(hgrn_lb):
    n = 2 * C_HEADS * C_DK
    out = pl.pallas_call(
        _lb_kernel,
        out_shape=jax.ShapeDtypeStruct((DEPTH, n), F32),
        name="hgrn_lb",
    )(hgrn_lb.reshape(DEPTH, n).astype(F32))
    return out.reshape(DEPTH, 2, 1, C_HEADS * C_DK)


def _rms(x, g):
    return x * lax.rsqrt(jnp.mean(x * x, axis=-1, keepdims=True) + EPS) * g


def _mod_rows(mod_ref, is_ctx):
    m = mod_ref[...]
    sel = jnp.where(is_ctx, m[1:2, :], m[0:1, :])
    return sel[:, 0:D_MODEL], sel[:, D_MODEL:2 * D_MODEL], sel[:, 2 * D_MODEL:3 * D_MODEL]


def _prenorm_kernel(x_ref, g_ref, mod_ref, h_ref, *, ctx_tiles):
    is_ctx = pl.program_id(0) < ctx_tiles
    shift, scale, _ = _mod_rows(mod_ref, is_ctx)
    h = _rms(x_ref[...], g_ref[...]) * (1.0 + scale) + shift
    h_ref[...] = h.astype(BF16)


def _prenorm(xs, g_pre, mod, li, L):
    R = xs.shape[0]
    tm = 256
    return pl.pallas_call(
        functools.partial(_prenorm_kernel, ctx_tiles=L // tm),
        grid=(R // tm,),
        in_specs=[
            pl.BlockSpec((tm, D_MODEL), lambda i: (i, 0)),
            pl.BlockSpec((None, 1, D_MODEL), lambda i: (li, 0, 0)),
            pl.BlockSpec((None, 8, 3 * D_MODEL), lambda i: (li, 0, 0)),
        ],
        out_specs=pl.BlockSpec((tm, D_MODEL), lambda i: (i, 0)),
        out_shape=jax.ShapeDtypeStruct((R, D_MODEL), BF16),
        compiler_params=_cparams(("arbitrary",)),
        name="prenorm",
    )(xs, g_pre, mod)


def _proj_kernel(h_ref, w_ref, o_ref):
    o_ref[...] = _dot(h_ref[...], w_ref[...]).astype(o_ref.dtype)


def _proj(h, w, li, tn):
    R = h.shape[0]
    n = w.shape[2]
    tm = _pick_tile(R, 1408)
    return pl.pallas_call(
        _proj_kernel,
        grid=(R // tm, n // tn),
        in_specs=[
            pl.BlockSpec((tm, D_MODEL), lambda i, j: (i, 0)),
            pl.BlockSpec((None, D_MODEL, tn), lambda i, j: (li, 0, j)),
        ],
        out_specs=pl.BlockSpec((tm, tn), lambda i, j: (i, j)),
        out_shape=jax.ShapeDtypeStruct((R, n), BF16),
        compiler_params=_cparams(("arbitrary", "arbitrary")),
        name="proj",
    )(h, w)


def _rope_tile(t, cos, sin):
    lane = lax.broadcasted_iota(jnp.int32, t.shape, 1)
    first = (lane & 16) == 0
    partner = jnp.where(first, pltpu.roll(t, LANES - 16, 1), pltpu.roll(t, 16, 1))
    return t * cos + partner * sin


def _prep_kernel(aq_ref, kv_ref, cq_ref, cos_ref, sin_ref, qn_ref, wuq_ref, kn_ref, wukv_ref,
                 aq_o, ak_o, q_o, k_o, v_o):
    cos = cos_ref[...]
    sin = sin_ref[...]
    a_scale = A_HEAD_DIM ** -0.5
    for t in range(A_HEADS * A_HEAD_DIM // LANES):
        sl = slice(t * LANES, (t + 1) * LANES)
        aq_o[:, sl] = (_rope_tile(aq_ref[:, sl].astype(F32), cos, sin) * a_scale).astype(BF16)
    ak_o[...] = _rope_tile(kv_ref[:, 0:128].astype(F32), cos, sin).astype(BF16)

    ckv = kv_ref[:, 256:384].astype(F32)
    ckv = _rms(ckv, kn_ref[...])
    kv = _dot(ckv.astype(BF16), wukv_ref[...])
    krt = kv_ref[:, 384:512].astype(F32)
    krt = jnp.where(lax.broadcasted_iota(jnp.int32, krt.shape, 1) < B_ROPE, krt, 0.0)
    kr = _rope_tile(krt, cos, sin).astype(BF16)
    for h in range(B_HEADS):
        k_o[h, :, 0:128] = kv[:, 256 * h:256 * h + 128].astype(BF16)
        k_o[h, :, 128:256] = kr
        v_o[h, :, 0:128] = kv[:, 256 * h + 128:256 * h + 256].astype(BF16)
        v_o[h, :, 128:256] = jnp.ones((kv.shape[0], 128), BF16)

    cq = cq_ref[...].astype(F32)
    cq = cq * lax.rsqrt(jnp.sum(cq * cq, axis=-1, keepdims=True) * (1.0 / B_Q_LORA) + EPS) * qn_ref[...]
    q = _dot(cq.astype(BF16), wuq_ref[...])
    b_scale = (B_NOPE + B_ROPE) ** -0.5 * LOG2E
    for h in range(B_HEADS):
        q_o[h, :, 0:128] = (q[:, 256 * h:256 * h + 128] * b_scale).astype(BF16)
        q_o[h, :, 128:256] = (_rope_tile(q[:, 256 * h + 128:256 * h + 256], cos, sin) * b_scale).astype(BF16)


def _prep(p1, pa, cos, sin, qn, wuq, kn, wukv, li):
    R = p1.shape[0]
    tm = 256
    row = lambda i: (i, 0)
    return pl.pallas_call(
        _prep_kernel,
        grid=(R // tm,),
        in_specs=[
            pl.BlockSpec((tm, 512), lambda i: (i, PA_AQ // 512)),
            pl.BlockSpec((tm, 512), row),
            pl.BlockSpec((tm, 512), lambda i: (i, PA_CQ // 512)),
            pl.BlockSpec((tm, LANES), row),
            pl.BlockSpec((tm, LANES), row),
            pl.BlockSpec((None, 1, 512), lambda i: (li, 0, 0)),
            pl.BlockSpec((None, 512, 1024), lambda i: (li, 0, 0)),
            pl.BlockSpec((None, 1, B_KV_LORA), lambda i: (li, 0, 0)),
            pl.BlockSpec((None, B_KV_LORA, 1024), lambda i: (li, 0, 0)),
        ],
        out_specs=[
            pl.BlockSpec((tm, 512), row),
            pl.BlockSpec((tm, LANES), row),
            pl.BlockSpec((B_HEADS, tm, 256), lambda i: (0, i, 0)),
            pl.BlockSpec((B_HEADS, tm, 256), lambda i: (0, i, 0)),
            pl.BlockSpec((B_HEADS, tm, 256), lambda i: (0, i, 0)),
        ],
        out_shape=[
            jax.ShapeDtypeStruct((R, 512), BF16),
            jax.ShapeDtypeStruct((R, LANES), BF16),
            jax.ShapeDtypeStruct((B_HEADS, R, 256), BF16),
            jax.ShapeDtypeStruct((B_HEADS, R, 256), BF16),
            jax.ShapeDtypeStruct((B_HEADS, R, 256), BF16),
        ],
        compiler_params=_cparams(("arbitrary",)),
        name="prep",
    )(pa, p1, pa, cos, sin, qn, wuq, kn, wukv)


def _swap_halves(t):
    return jnp.concatenate([t[:, 64:], t[:, :64]], axis=1)


def _win_kernel(sink_ref, q_ref, kp_ref, ks_ref, kn_ref, vp_ref, vs_ref, vn_ref, kc_ref, vc_ref, o_ref,
                *, L, R):
    i = pl.program_id(0)
    q = q_ref[...]
    kl = jnp.concatenate([kp_ref[...], ks_ref[...], kn_ref[...]], axis=0)
    vl = jnp.concatenate([vp_ref[...], vs_ref[...], vn_ref[...]], axis=0)
    kc = kc_ref[...]
    vc = vc_ref[...]
    lo = lax.broadcasted_iota(jnp.int32, (BLK, LANES), 1) < 64
    zero = jnp.zeros((BLK, LANES), BF16)

    rows = 4 * BLK
    rr = lax.broadcasted_iota(jnp.int32, (rows, 3 * BLK), 0)
    cc = lax.broadcasted_iota(jnp.int32, (rows, 3 * BLK), 1)
    qpos = (rr & (BLK - 1)) + i * BLK
    kpos = cc + (i - 1) * BLK
    ok = ((kpos >= L) & (kpos < R) & (qpos >= L)
          & (kpos - qpos <= WINDOW) & (qpos - kpos <= WINDOW))
    hrow = lax.broadcasted_iota(jnp.int32, (rows, 1), 0) // BLK

    for g in range(2):
        t0 = q[:, 256 * g:256 * g + 128]
        t1 = q[:, 256 * g + 128:256 * g + 256]
        if g == 0:
            parts = [jnp.where(lo, t0, zero), jnp.where(lo, _swap_halves(t0), zero),
                     jnp.where(lo, t1, zero), jnp.where(lo, _swap_halves(t1), zero)]
        else:
            parts = [jnp.where(lo, zero, _swap_halves(t0)), jnp.where(lo, zero, t0),
                     jnp.where(lo, zero, _swap_halves(t1)), jnp.where(lo, zero, t1)]
        qs = jnp.concatenate(parts, axis=0)
        s_l = jnp.where(ok, _dot_nt(qs, kl), -jnp.inf)
        s_c = _dot_nt(qs, kc)
        sink = jnp.zeros((rows, 1), F32)
        for j in range(4):
            sink = jnp.where(hrow == j, sink_ref[4 * g + j], sink)
        m = jnp.maximum(jnp.maximum(jnp.max(s_l, axis=-1, keepdims=True),
                                    jnp.max(s_c, axis=-1, keepdims=True)), sink)
        p_l = jnp.exp(s_l - m)
        p_c = jnp.exp(s_c - m)
        den = (jnp.sum(p_l, axis=-1, keepdims=True) + jnp.sum(p_c, axis=-1, keepdims=True)
               + jnp.exp(sink - m))
        o = (_dot(p_l.astype(BF16), vl) + _dot(p_c.astype(BF16), vc)) / den
        for pair in range(2):
            a = o[2 * pair * BLK:(2 * pair + 1) * BLK]
            b = o[(2 * pair + 1) * BLK:(2 * pair + 2) * BLK]
            if g == 0:
                tile = jnp.where(lo, a, _swap_halves(b))
            else:
                tile = jnp.where(lo, _swap_halves(a), b)
            c0 = 256 * g + 128 * pair
            o_ref[:, c0:c0 + 128] = tile.astype(BF16)


def _window_attn(sink, aq, ak, p, L):
    R = aq.shape[0]
    nb = R // BLK
    prev = lambda c: (lambda i: (jnp.maximum(i - 1, 0), c))
    this = lambda c: (lambda i: (i, c))
    nxt = lambda c: (lambda i: (jnp.minimum(i + 1, nb - 1), c))
    return pl.pallas_call(
        functools.partial(_win_kernel, L=L, R=R),
        grid=(nb,),
        in_specs=[
            pl.BlockSpec(memory_space=pltpu.SMEM),
            pl.BlockSpec((BLK, 512), lambda i: (i, 0)),
            pl.BlockSpec((BLK, LANES), prev(0)),
            pl.BlockSpec((BLK, LANES), this(0)),
            pl.BlockSpec((BLK, LANES), nxt(0)),
            pl.BlockSpec((BLK, LANES), prev(1)),
            pl.BlockSpec((BLK, LANES), this(1)),
            pl.BlockSpec((BLK, LANES), nxt(1)),
            pl.BlockSpec((L, LANES), lambda i: (0, 0)),
            pl.BlockSpec((L, LANES), lambda i: (0, 1)),
        ],
        out_specs=pl.BlockSpec((BLK, 512), lambda i: (i, 0)),
        out_shape=jax.ShapeDtypeStruct((R, 512), BF16),
        compiler_params=_cparams(("arbitrary",)),
        name="window_attn",
    )(sink, aq, ak, ak, ak, p, p, p, ak, p)


def _softmax_out(acc):
    return (acc[:, 0:B_VDIM] / acc[:, B_VDIM:2 * B_VDIM]).astype(BF16)


def _mla_kernel(qn_ref, q0_ref, k_ref, v_ref, o_ref, s0_ref, s1_ref, m0_ref, m1_ref, *, L, tq, tk):
    i = pl.program_id(1)
    nk = k_ref.shape[0] // tk

    @pl.when(i == 0)
    def _():
        s1_ref[...] = jnp.zeros_like(s1_ref)
        m1_ref[...] = jnp.zeros_like(m1_ref)

    def step(sw_ref, mw_ref, sr_ref, mr_ref):
        q = qn_ref[...]
        m_prev = mr_ref[...]
        mrun = jnp.full((tq, LANES), -jnp.inf, F32)
        acc = jnp.zeros((tq, 2 * B_VDIM), F32)
        for c in range(nk):
            rows = slice(c * tk, (c + 1) * tk)
            s = _dot_nt(q, k_ref[rows, :])
            sw_ref[c] = s
            for t in range(tk // LANES):
                mrun = jnp.maximum(mrun, s[:, t * LANES:(t + 1) * LANES])
            sp = sr_ref[c]
            p = jnp.concatenate(
                [jnp.exp2(sp[:, t * LANES:(t + 1) * LANES] - m_prev) for t in range(tk // LANES)], axis=1)
            acc = acc + _dot(p.astype(BF16), v_ref[rows, :])
        mw_ref[...] = jnp.broadcast_to(jnp.max(mrun, axis=-1, keepdims=True), (tq, LANES))
        o_ref[...] = _softmax_out(acc)

    @pl.when(i % 2 == 0)
    def _():
        step(s0_ref, m0_ref, s1_ref, m1_ref)

    @pl.when(i % 2 == 1)
    def _():
        step(s1_ref, m1_ref, s0_ref, m0_ref)

    @pl.when(i == 0)
    def _():
        s = _dot_nt(q0_ref[...], k_ref[0:L, :])
        p = jnp.exp2(s - jnp.max(s, axis=-1, keepdims=True))
        o_ref[...] = _softmax_out(_dot(p.astype(BF16), v_ref[0:L, :]))


def _mla_attn(q, k, v, L):
    R = q.shape[1]
    tq = 256
    tk = 256
    assert L == tq
    nt = R // tq
    return pl.pallas_call(
        functools.partial(_mla_kernel, L=L, tq=tq, tk=tk),
        grid=(B_HEADS, nt),
        in_specs=[
            pl.BlockSpec((None, tq, 256), lambda h, i: (h, jnp.minimum(i + 1, nt - 1), 0)),
            pl.BlockSpec((None, tq, 256), lambda h, i: (h, 0, 0)),
            pl.BlockSpec((None, R, 256), lambda h, i: (h, 0, 0)),
            pl.BlockSpec((None, R, 256), lambda h, i: (h, 0, 0)),
        ],
        out_specs=pl.BlockSpec((tq, B_VDIM), lambda h, i: (i, h)),
        out_shape=jax.ShapeDtypeStruct((R, B_HEADS * B_VDIM), BF16),
        scratch_shapes=[
            pltpu.VMEM((R // tk, tq, tk), F32),
            pltpu.VMEM((R // tk, tq, tk), F32),
            pltpu.VMEM((tq, LANES), F32),
            pltpu.VMEM((tq, LANES), F32),
        ],
        compiler_params=_cparams(("arbitrary", "arbitrary")),
        name="mla_attn",
    )(q, q, k, v)


LEVELS = (64, 32, 16, 8, 4, 2, 1)


def _split3(x):
    hi = x.astype(BF16)
    r = x - hi.astype(F32)
    mid = r.astype(BF16)
    lo = (r - mid.astype(F32)).astype(BF16)
    return hi, mid, lo


def _hgrn_kernel(zf_ref, vf_ref, qf_ref, zb_ref, vb_ref, qb_ref, lb_ref, of_ref, ob_ref, st_ref):
    n = pl.program_id(0)

    @pl.when(n == 0)
    def _():
        st_ref[...] = jnp.zeros_like(st_ref)

    half = C_HEADS * C_DK
    width = 2 * half
    nh = 2 * C_HEADS
    heads = [slice(h * C_DK, (h + 1) * C_DK) for h in range(nh)]
    r0 = lax.broadcasted_iota(jnp.int32, (BLK, BLK), 0)
    c0 = lax.broadcasted_iota(jnp.int32, (BLK, BLK), 1)
    lt = (r0 >= c0).astype(BF16)
    rev = (r0 + c0 == BLK - 1).astype(BF16)
    ones = jnp.ones((BLK, BLK), BF16)
    rw = lax.broadcasted_iota(jnp.int32, (BLK, width), 0)
    sub8 = lax.broadcasted_iota(jnp.int32, (BLK // 8, 8, width), 1)

    z = jnp.concatenate([zf_ref[...].astype(F32), _dot(rev, zb_ref[...])], axis=1)
    v = jnp.concatenate([vf_ref[...].astype(F32), _dot(rev, vb_ref[...])], axis=1)
    vb = v.astype(BF16)
    qp = jnp.concatenate([qf_ref[...].astype(F32), _dot(rev, qb_ref[...])], axis=1)
    lb = jnp.concatenate([lb_ref[0], lb_ref[1]], axis=1)

    ls = jnp.minimum(z, 0.0) - jnp.log(1.0 + jnp.exp(-jnp.abs(z)))
    a1 = jnp.log(lb)
    a2 = jnp.log1p(-lb) + ls
    lf = jnp.maximum(a1, a2) + jnp.log(1.0 + jnp.exp(-jnp.abs(a1 - a2)))
    k = (1.0 - lb) * (1.0 / (1.0 + jnp.exp(z)))
    q = qp * _sigmoid(qp) * (C_DK ** -0.5)

    hi, mid, lo = _split3(lf * LOG2E)
    b = _dot(lt, hi) + _dot(lt, mid) + _dot(lt, lo)

    sts = [st_ref[h] for h in range(nh)]
    qe = (q * jnp.exp2(b)).astype(BF16)
    outs = [_dot_nt(qe[:, sl], st.astype(BF16)) for sl, st in zip(heads, sts)]

    b8 = b.reshape(BLK // 8, 8, width)
    atts = [jnp.zeros((BLK, BLK), F32) for _ in heads]
    for s in LEVELS:
        if s >= 8:
            g = BLK // (2 * s)
            e = jnp.broadcast_to(b.reshape(g, 2 * s, width)[:, s - 1:s, :], (g, 2 * s, width))
        elif s == 4:
            e = jnp.broadcast_to(b8[:, 3:4, :], b8.shape)
        elif s == 2:
            e = jnp.where(sub8 < 4, jnp.broadcast_to(b8[:, 1:2, :], b8.shape),
                          jnp.broadcast_to(b8[:, 5:6, :], b8.shape))
        else:
            e = jnp.where((sub8 & 1) == 0, b8, pltpu.roll(b8, 1, 1))
        w = jnp.exp2(-jnp.abs(b - e.reshape(BLK, width)))
        right = (rw & s) != 0
        qt = jnp.where(right, q * w, 0.0).astype(BF16)
        kt = jnp.where(right, 0.0, k * w).astype(BF16)
        pair = (r0 // (2 * s)) == (c0 // (2 * s))
        atts = [jnp.where(pair, _dot_nt(qt[:, sl], kt[:, sl]), att) for sl, att in zip(heads, atts)]
    dg = (q * k).astype(BF16)
    outs = [o + _dot(att.astype(BF16), vb[:, sl]) + _dot(dg[:, sl], ones) * v[:, sl]
            for sl, o, att in zip(heads, outs, atts)]

    bl = b[BLK - 1:BLK, :]
    kl = (k * jnp.exp2(bl - b)).astype(BF16)
    dec = jnp.exp2(bl)
    for h, sl in enumerate(heads):
        st_ref[h] = sts[h] * dec[:, sl] + _dot(v[:, sl].T.astype(BF16), kl[:, sl])

    of_ref[...] = jnp.concatenate(outs[:C_HEADS], axis=1)
    o = jnp.concatenate(outs[C_HEADS:], axis=1)
    ohi = o.astype(BF16)
    olo = (o - ohi.astype(F32)).astype(BF16)
    ob_ref[...] = _dot(rev, ohi) + _dot(rev, olo)


def _hgrn(pa, pc, lb, li, L):
    R = pa.shape[0]
    nb = R // BLK
    nc = L // BLK

    def back(n):
        return jnp.where(n < nc, nc - 1 - n, nb - 1 - (n - nc))

    fwd = lambda c: (lambda n: (n, c))
    bwd = lambda c: (lambda n: (back(n), c))
    return pl.pallas_call(
        _hgrn_kernel,
        grid=(nb,),
        in_specs=[
            pl.BlockSpec((BLK, 512), fwd(PA_CFF // 512)),
            pl.BlockSpec((BLK, 512), fwd(PA_CI // 512)),
            pl.BlockSpec((BLK, 512), fwd(PC_HQ // 512)),
            pl.BlockSpec((BLK, 512), bwd(PA_CFF // 512 + 1)),
            pl.BlockSpec((BLK, 512), bwd(PA_CI // 512)),
            pl.BlockSpec((BLK, 512), bwd(PC_HQ // 512)),
            pl.BlockSpec((None, 2, 1, 512), lambda n: (li, 0, 0, 0)),
        ],
        out_specs=[pl.BlockSpec((BLK, 512), fwd(0)), pl.BlockSpec((BLK, 512), bwd(0))],
        out_shape=[jax.ShapeDtypeStruct((R, 512), F32), jax.ShapeDtypeStruct((R, 512), F32)],
        scratch_shapes=[pltpu.VMEM((2 * C_HEADS, C_DK, C_DK), F32)],
        compiler_params=_cparams(("arbitrary",)),
        name="hgrn",
    )(pa, pa, pc, pa, pa, pc, lb)


POOL_TM = 256
POOL_HALO = 16


def _pool_kernel(up_ref, us_ref, un_ref, w_ref, sc_ref, o_ref, *, L, R):
    i = pl.program_id(0)
    tm, halo = POOL_TM, POOL_HALO
    u3 = jnp.concatenate([up_ref[...], us_ref[...], un_ref[...]], axis=0)
    us = us_ref[...].astype(F32)
    r = lax.broadcasted_iota(jnp.int32, (tm, tm + 2 * halo), 0) + i * tm
    s = lax.broadcasted_iota(jnp.int32, (tm, tm + 2 * halo), 1) + (i * tm - halo)
    seq_lo = jnp.where(r < L, 0, L)
    seq_hi = jnp.where(r < L, L, R)
    r1 = lax.broadcasted_iota(jnp.int32, (tm, LANES), 0) + i * tm
    seq_lo1 = jnp.where(r1 < L, 0, L)
    seq_hi1 = jnp.where(r1 < L, L, R)
    for g, w in enumerate(POOL_WINDOWS):
        sl = slice(g * LANES, (g + 1) * LANES)
        lo = jnp.maximum(r - w // 2, seq_lo)
        hi = jnp.minimum(r + w - w // 2, seq_hi)
        band = ((s >= lo) & (s < hi)).astype(BF16)
        cnt = (jnp.minimum(r1 + w - w // 2, seq_hi1) - jnp.maximum(r1 - w // 2, seq_lo1)).astype(F32)
        mean = _dot(band, u3[:, sl]) / cnt
        d = (mean - us[:, sl]).astype(BF16)
        o_ref[:, sl] = (_dot(d, w_ref[g]) * sc_ref[:, sl]).astype(BF16)


def _pool(pc, w_pool, pool_scale, li, L):
    R = pc.shape[0]
    tm, halo = POOL_TM, POOL_HALO
    assert max(POOL_WINDOWS) // 2 <= halo and tm % halo == 0
    nb = R // tm
    hb = tm // halo
    cb = PC_DX // 512
    return pl.pallas_call(
        functools.partial(_pool_kernel, L=L, R=R),
        grid=(nb,),
        in_specs=[
            pl.BlockSpec((halo, 512), lambda i: (jnp.maximum(i * hb - 1, 0), cb)),
            pl.BlockSpec((tm, 512), lambda i: (i, cb)),
            pl.BlockSpec((halo, 512), lambda i: (jnp.minimum((i + 1) * hb, R // halo - 1), cb)),
            pl.BlockSpec((None, len(POOL_WINDOWS), LANES, LANES), lambda i: (li, 0, 0, 0)),
            pl.BlockSpec((None, 1, 512), lambda i: (li, 0, 0)),
        ],
        out_specs=pl.BlockSpec((tm, 512), lambda i: (i, 0)),
        out_shape=jax.ShapeDtypeStruct((R, 512), BF16),
        compiler_params=_cparams(("arbitrary",)),
        name="pool",
    )(pc, pc, pc, w_pool, pool_scale)


def _merge_kernel(oa_ref, ob_ref, ohf_ref, ohb_ref, od_ref, gt_ref, mg0_ref, mg1_ref, mg2_ref, mg3_ref,
                  wbr_ref, wout_ref, hn_ref, x_ref, mod_ref, gpost_ref, gnext_ref, modn_ref,
                  xo_ref, ho_ref, *, ctx_tiles):
    is_ctx = pl.program_id(0) < ctx_tiles
    hs = ohf_ref[...] + ohb_ref[...]
    hn = hn_ref[...]
    oh = jnp.concatenate(
        [_rms(hs[:, h * C_DK:(h + 1) * C_DK], hn[:, h * C_DK:(h + 1) * C_DK]) for h in range(C_HEADS)], axis=1)
    outs = (oa_ref[...].astype(F32), ob_ref[...].astype(F32), oh, od_ref[...].astype(F32))
    mgs = (mg0_ref, mg1_ref, mg2_ref, mg3_ref)
    merged = None
    for nbr in range(N_BRANCH):
        gate = gt_ref[:, nbr * BRANCH_W:(nbr + 1) * BRANCH_W].astype(F32)
        ys = (outs[nbr] * (gate * _sigmoid(gate))).astype(BF16)
        yb = _dot(ys, wbr_ref[nbr])
        term = _sigmoid(mgs[nbr][...].astype(F32)) * yb
        merged = term if merged is None else merged + term
    y = _dot(merged.astype(BF16), wout_ref[...])
    _, _, gate_r = _mod_rows(mod_ref, is_ctx)
    xn = x_ref[...] + gate_r * _rms(y, gpost_ref[...])
    xo_ref[...] = xn
    shift_n, scale_n, _ = _mod_rows(modn_ref, is_ctx)
    ho_ref[...] = (_rms(xn, gnext_ref[...]) * (1.0 + scale_n) + shift_n).astype(BF16)


def _merge(oa, ob, ohf, ohb, od, pb, wbr, wout, hnorm, xs, mod, g_post, g_pre, li, L):
    R = xs.shape[0]
    tm = 256
    ln = min(li + 1, DEPTH - 1)
    row = lambda i: (i, 0)
    gcol = 0
    mcol = 1
    return pl.pallas_call(
        functools.partial(_merge_kernel, ctx_tiles=L // tm),
        grid=(R // tm,),
        in_specs=[
            pl.BlockSpec((tm, 512), row),
            pl.BlockSpec((tm, 512), row),
            pl.BlockSpec((tm, 512), row),
            pl.BlockSpec((tm, 512), row),
            pl.BlockSpec((tm, 512), row),
            pl.BlockSpec((tm, D_MODEL), lambda i: (i, gcol)),
            pl.BlockSpec((tm, D_MODEL), lambda i: (i, mcol)),
            pl.BlockSpec((tm, D_MODEL), lambda i: (i, mcol + 1)),
            pl.BlockSpec((tm, D_MODEL), lambda i: (i, mcol + 2)),
            pl.BlockSpec((tm, D_MODEL), lambda i: (i, mcol + 3)),
            pl.BlockSpec((None, N_BRANCH, BRANCH_W, D_MODEL), lambda i: (li, 0, 0, 0)),
            pl.BlockSpec((None, D_MODEL, D_MODEL), lambda i: (li, 0, 0)),
            pl.BlockSpec((None, 1, 512), lambda i: (li, 0, 0)),
            pl.BlockSpec((tm, D_MODEL), row),
            pl.BlockSpec((None, 8, 3 * D_MODEL), lambda i: (li, 0, 0)),
            pl.BlockSpec((None, 1, D_MODEL), lambda i: (li, 0, 0)),
            pl.BlockSpec((None, 1, D_MODEL), lambda i: (ln, 0, 0)),
            pl.BlockSpec((None, 8, 3 * D_MODEL), lambda i: (ln, 0, 0)),
        ],
        out_specs=[pl.BlockSpec((tm, D_MODEL), row), pl.BlockSpec((tm, D_MODEL), row)],
        out_shape=[jax.ShapeDtypeStruct((R, D_MODEL), F32), jax.ShapeDtypeStruct((R, D_MODEL), BF16)],
        compiler_params=_cparams(("arbitrary",)),
        name="merge",
    )(oa, ob, ohf, ohb, od, pb, pb, pb, pb, pb, wbr, wout, hnorm, xs, mod, g_post, g_pre, mod)


def _rope_tables(T, L):
    rows = jnp.repeat(jnp.arange(T // GRID_W, dtype=jnp.int32), GRID_W)
    cols = jnp.tile(jnp.arange(GRID_W, dtype=jnp.int32), T // GRID_W)
    half = 16
    freqs = ROPE_BASE ** (-jnp.arange(half, dtype=F32) / half)
    ar = rows.astype(F32)[:, None] * freqs[None, :]
    ac = cols.astype(F32)[:, None] * freqs[None, :]
    cos = jnp.concatenate([jnp.cos(ar), jnp.cos(ar), jnp.cos(ac), jnp.cos(ac)], axis=1)
    sin = jnp.concatenate([-jnp.sin(ar), jnp.sin(ar), -jnp.sin(ac), jnp.sin(ac)], axis=1)
    cos = jnp.concatenate([jnp.ones((L, 64), F32), cos], axis=0)
    sin = jnp.concatenate([jnp.zeros((L, 64), F32), sin], axis=0)
    return jnp.tile(cos, (1, 2)), jnp.tile(sin, (1, 2))


def kernel(x, c, ctx, c_ctx, w_ada, b_ada, g_pre, g_post, w_in, a_sink, mla_q_norm, w_uq, mla_kv_norm,
           w_ukv, hgrn_lb, hgrn_norm, w_pool, pool_scale, w_branch, w_out):
    assert x.shape[0] == 1 and ctx.shape[0] == 1
    T = x.shape[1]
    L = ctx.shape[1]
    assert T % 256 == 0 and L % 256 == 0 and T % GRID_W == 0

    w1 = w_in[:, :, :O_P1_END].astype(BF16)
    wa = jnp.pad(w_in[:, :, O_PA[0]:O_PA[1]].astype(BF16), ((0, 0), (0, 0), (0, 128)))
    wc = w_in[:, :, O_PC[0]:O_PC[1]].astype(BF16)
    wb = w_in[:, :, O_PB:].astype(BF16)
    wuq = w_uq.astype(BF16).reshape(DEPTH, B_Q_LORA, B_HEADS, B_NOPE + B_ROPE)
    wuq = jnp.pad(wuq, ((0, 0), (0, 512 - B_Q_LORA), (0, 0), (0, 256 - B_NOPE - B_ROPE))).reshape(DEPTH, 512, 1024)
    qn = jnp.pad(mla_q_norm.astype(F32), ((0, 0), (0, 512 - B_Q_LORA))).reshape(DEPTH, 1, 512)
    kn = mla_kv_norm.astype(F32).reshape(DEPTH, 1, B_KV_LORA)
    wukv = w_ukv.astype(BF16)
    wpool = w_pool.astype(BF16)
    pscale = pool_scale.astype(F32).reshape(DEPTH, 1, BRANCH_W)
    wbr = w_branch.astype(BF16)
    wout = w_out.astype(BF16)
    hnorm = hgrn_norm.astype(F32).reshape(DEPTH, 1, C_HEADS * C_DK)
    gpre = g_pre.astype(F32).reshape(DEPTH, 1, D_MODEL)
    gpost = g_post.astype(F32).reshape(DEPTH, 1, D_MODEL)

    cos, sin = _rope_tables(T, L)

    cc = jnp.concatenate([c.astype(F32), c_ctx.astype(F32)[None, :], jnp.zeros((6, D_MODEL), F32)], axis=0)
    mod = _ada_all(cc, w_ada, b_ada)
    lb = _lower_bounds(hgrn_lb)

    xs = jnp.concatenate([ctx[0], x[0]], axis=0).astype(F32)
    h = _prenorm(xs, gpre, mod, 0, L)
    for li in range(DEPTH):
        p1 = _proj(h, w1, li, 512)
        pa = _proj(h, wa, li, 1280)
        pc = _proj(h, wc, li, 1024)
        pb = _proj(h, wb, li, 1024)
        aq, ak, mq, mk, mv = _prep(p1, pa, cos, sin, qn, wuq, kn, wukv, li)
        oa = _window_attn(a_sink[li].astype(F32), aq, ak, p1, L)
        ob = _mla_attn(mq, mk, mv, L)
        ohf, ohb = _hgrn(pa, pc, lb, li, L)
        od = _pool(pc, wpool, pscale, li, L)
        xs, h = _merge(oa, ob, ohf, ohb, od, pb, wbr, wout, hnorm, xs, mod, gpost, gpre, li, L)
    return xs[L:][None].astype(x.dtype)
```

```python
import functools

import jax
import jax.numpy as jnp
from jax import lax
from jax.experimental import pallas as pl
from jax.experimental.pallas import tpu as pltpu

F32 = jnp.float32
BF16 = jnp.bfloat16

D_MODEL = 2048
DEPTH = 4
GRID_W = 64
EPS = 1e-6
ROPE_BASE = 10000.0
N_BRANCH = 4
BRANCH_W = 512
A_HEADS = 8
A_HEAD_DIM = 64
WINDOW = 128
B_HEADS = 4
B_NOPE = 128
B_ROPE = 64
B_VDIM = 128
B_Q_LORA = 384
B_KV_LORA = 128
C_HEADS = 4
C_DK = 128
POOL_WINDOWS = (2, 4, 8, 16)

O_P1_END = 512
O_PA = 448
PA_WIDTH = 2560
O_PC = 2880
O_PB = 3904
O_END = 14144
PA_CFF = 0
PA_CI = 1024
PA_AQ = 1536
PA_CQ = 2048
PC_HQ = 0
PC_DX = 512

BLK = 128
LANES = 128
VMEM_LIMIT = 56 * 1024 * 1024

NT_DIMS = (((1,), (1,)), ((), ()))
LOG2E = 1.4426950408889634


def _cparams(sem):
    return pltpu.CompilerParams(dimension_semantics=sem, vmem_limit_bytes=VMEM_LIMIT)


def _pick_tile(n, cap, mult=BLK):
    best = mult
    t = mult
    while t <= min(n, cap):
        if n % t == 0:
            best = t
        t += mult
    return best


def _dot(a, b):
    return jnp.dot(a, b, preferred_element_type=F32)


def _dot_nt(a, b):
    return lax.dot_general(a, b, NT_DIMS, preferred_element_type=F32)


def _sigmoid(x):
    return 0.5 * jnp.tanh(0.5 * x) + 0.5


def _ada_kernel(cc_ref, w_ref, b_ref, o_ref):
    cc = cc_ref[...]
    s = (cc * _sigmoid(cc)).astype(BF16)
    o_ref[...] = _dot(s, w_ref[...].astype(BF16)) + b_ref[...]


def _ada_all(cc, w_ada, b_ada):
    tn = 1024
    n3 = 3 * D_MODEL
    return pl.pallas_call(
        _ada_kernel,
        grid=(DEPTH, n3 // tn),
        in_specs=[
            pl.BlockSpec((8, D_MODEL), lambda l, j: (0, 0)),
            pl.BlockSpec((None, D_MODEL, tn), lambda l, j: (l, 0, j)),
            pl.BlockSpec((None, 1, tn), lambda l, j: (l, 0, j)),
        ],
        out_specs=pl.BlockSpec((None, 8, tn), lambda l, j: (l, 0, j)),
        out_shape=jax.ShapeDtypeStruct((DEPTH, 8, n3), F32),
        compiler_params=_cparams(("arbitrary", "arbitrary")),
        name="adaln",
    )(cc, w_ada, b_ada.reshape(DEPTH, 1, n3))


def _lb_kernel(x_ref, o_ref):
    x = x_ref[...]
    m = jnp.max(x, axis=0, keepdims=True)
    e = jnp.exp(x - m)
    sm = e / jnp.sum(e, axis=0, keepdims=True)
    rows = []
    run = sm[0:1]
    first = run
    for l in range(DEPTH):
        if l > 0:
            run = run + sm[l:l + 1]
        rows.append(run - first)
    o_ref[...] = jnp.concatenate(rows, axis=0)


def _lower_bounds(hgrn_lb):
    n = 2 * C_HEADS * C_DK
    out = pl.pallas_call(
        _lb_kernel,
        out_shape=jax.ShapeDtypeStruct((DEPTH, n), F32),
        name="hgrn_lb",
    )(hgrn_lb.reshape(DEPTH, n).astype(F32))
    return out.reshape(DEPTH, 2, 1, C_HEADS * C_DK)


def _rms(x, g):
    return x * lax.rsqrt(jnp.mean(x * x, axis=-1, keepdims=True) + EPS) * g


def _mod_rows(mod_ref, is_ctx):
    m = mod_ref[...]
    sel = jnp.where(is_ctx, m[1:2, :], m[0:1, :])
    return sel[:, 0:D_MODEL], sel[:, D_MODEL:2 * D_MODEL], sel[:, 2 * D_MODEL:3 * D_MODEL]


def _prenorm_kernel(x_ref, g_ref, mod_ref, h_ref, *, ctx_tiles):
    is_ctx = pl.program_id(0) < ctx_tiles
    shift, scale, _ = _mod_rows(mod_ref, is_ctx)
    h = _rms(x_ref[...], g_ref[...]) * (1.0 + scale) + shift
    h_ref[...] = h.astype(BF16)


def _prenorm(xs, g_pre, mod, li, L):
    R = xs.shape[0]
    tm = 256
    return pl.pallas_call(
        functools.partial(_prenorm_kernel, ctx_tiles=L // tm),
        grid=(R // tm,),
        in_specs=[
            pl.BlockSpec((tm, D_MODEL), lambda i: (i, 0)),
            pl.BlockSpec((None, 1, D_MODEL), lambda i: (li, 0, 0)),
            pl.BlockSpec((None, 8, 3 * D_MODEL), lambda i: (li, 0, 0)),
        ],
        out_specs=pl.BlockSpec((tm, D_MODEL), lambda i: (i, 0)),
        out_shape=jax.ShapeDtypeStruct((R, D_MODEL), BF16),
        compiler_params=_cparams(("arbitrary",)),
        name="prenorm",
    )(xs, g_pre, mod)


PREV_W = 256


def _proj_kernel(h_ref, wp_ref, wc_ref, o_ref, wbf_ref, *, first, tn):
    @pl.when(pl.program_id(1) == 0)
    def _():
        if first is None:
            wbf_ref[...] = wc_ref[...].astype(BF16)
        else:
            lo = lax.broadcasted_iota(jnp.int32, (D_MODEL, LANES), 1) < LANES // 2
            n_prev = PREV_W // LANES

            def swapped(t):
                ref, idx = (wp_ref, t) if t < n_prev else (wc_ref, t - n_prev)
                return pltpu.roll(ref[:, idx * LANES:(idx + 1) * LANES], LANES // 2, 1)

            cur = swapped(first)
            for k in range(tn // LANES):
                nxt = swapped(first + k + 1)
                wbf_ref[:, k * LANES:(k + 1) * LANES] = jnp.where(lo, cur, nxt).astype(BF16)
                cur = nxt

    o_ref[...] = _dot(h_ref[...], wbf_ref[...]).astype(o_ref.dtype)


def _proj(h, w_in, li, start, width, tn):
    R = h.shape[0]
    tm = _pick_tile(R, 1408)
    assert width % tn == 0 and tn % PREV_W == 0
    if start % LANES == 0:
        assert start % tn == 0
        first = None
        prev_idx = lambda j, i: (li, 0, 0)
        cur0 = start // tn
    else:
        assert start % LANES == LANES // 2
        bound = start + PREV_W - start % PREV_W
        assert bound % tn == 0
        first = (start % PREV_W) // LANES
        prev0 = start // PREV_W
        prev_idx = lambda j, i: (li, 0, prev0 + (tn // PREV_W) * j)
        cur0 = bound // tn
    return pl.pallas_call(
        functools.partial(_proj_kernel, first=first, tn=tn),
        grid=(width // tn, R // tm),
        in_specs=[
            pl.BlockSpec((tm, D_MODEL), lambda j, i: (i, 0)),
            pl.BlockSpec((None, D_MODEL, PREV_W), prev_idx),
            pl.BlockSpec((None, D_MODEL, tn), lambda j, i: (li, 0, cur0 + j)),
        ],
        out_specs=pl.BlockSpec((tm, tn), lambda j, i: (i, j)),
        out_shape=jax.ShapeDtypeStruct((R, width), BF16),
        scratch_shapes=[pltpu.VMEM((D_MODEL, tn), BF16)],
        compiler_params=_cparams(("arbitrary", "arbitrary")),
        name="proj",
    )(h, w_in, w_in)


def _rope_tile(t, cos, sin):
    lane = lax.broadcasted_iota(jnp.int32, t.shape, 1)
    first = (lane & 16) == 0
    partner = jnp.where(first, pltpu.roll(t, LANES - 16, 1), pltpu.roll(t, 16, 1))
    return t * cos + partner * sin


def _prep_kernel(aq_ref, kv_ref, cq_ref, cos_ref, sin_ref, qn_ref, wuq_ref, kn_ref, wukv_ref,
                 aq_o, ak_o, q_o, k_o, v_o):
    cos = cos_ref[...]
    sin = sin_ref[...]
    a_scale = A_HEAD_DIM ** -0.5
    for t in range(A_HEADS * A_HEAD_DIM // LANES):
        sl = slice(t * LANES, (t + 1) * LANES)
        aq_o[:, sl] = (_rope_tile(aq_ref[:, sl].astype(F32), cos, sin) * a_scale).astype(BF16)
    ak_o[...] = _rope_tile(kv_ref[:, 0:128].astype(F32), cos, sin).astype(BF16)

    ckv = kv_ref[:, 256:384].astype(F32)
    ckv = _rms(ckv, kn_ref[...])
    kv = _dot(ckv.astype(BF16), wukv_ref[...])
    krt = kv_ref[:, 384:512].astype(F32)
    krt = jnp.where(lax.broadcasted_iota(jnp.int32, krt.shape, 1) < B_ROPE, krt, 0.0)
    kr = _rope_tile(krt, cos, sin).astype(BF16)
    for h in range(B_HEADS):
        k_o[h, :, 0:128] = kv[:, 256 * h:256 * h + 128].astype(BF16)
        k_o[h, :, 128:256] = kr
        v_o[h, :, 0:128] = kv[:, 256 * h + 128:256 * h + 256].astype(BF16)
        v_o[h, :, 128:256] = jnp.ones((kv.shape[0], 128), BF16)

    cq = cq_ref[...].astype(F32)
    cq = jnp.where(lax.broadcasted_iota(jnp.int32, cq.shape, 1) < B_Q_LORA, cq, 0.0)
    cq = cq * lax.rsqrt(jnp.sum(cq * cq, axis=-1, keepdims=True) * (1.0 / B_Q_LORA) + EPS) * qn_ref[...]
    q = _dot(cq.astype(BF16), wuq_ref[...])
    b_scale = (B_NOPE + B_ROPE) ** -0.5 * LOG2E
    for h in range(B_HEADS):
        q_o[h, :, 0:128] = (q[:, 256 * h:256 * h + 128] * b_scale).astype(BF16)
        q_o[h, :, 128:256] = (_rope_tile(q[:, 256 * h + 128:256 * h + 256], cos, sin) * b_scale).astype(BF16)


def _prep(p1, pa, cos, sin, qn, wuq, kn, wukv, li):
    R = p1.shape[0]
    tm = 256
    row = lambda i: (i, 0)
    return pl.pallas_call(
        _prep_kernel,
        grid=(R // tm,),
        in_specs=[
            pl.BlockSpec((tm, 512), lambda i: (i, PA_AQ // 512)),
            pl.BlockSpec((tm, 512), row),
            pl.BlockSpec((tm, 512), lambda i: (i, PA_CQ // 512)),
            pl.BlockSpec((tm, LANES), row),
            pl.BlockSpec((tm, LANES), row),
            pl.BlockSpec((None, 1, 512), lambda i: (li, 0, 0)),
            pl.BlockSpec((None, 512, 1024), lambda i: (li, 0, 0)),
            pl.BlockSpec((None, 1, B_KV_LORA), lambda i: (li, 0, 0)),
            pl.BlockSpec((None, B_KV_LORA, 1024), lambda i: (li, 0, 0)),
        ],
        out_specs=[
            pl.BlockSpec((tm, 512), row),
            pl.BlockSpec((tm, LANES), row),
            pl.BlockSpec((B_HEADS, tm, 256), lambda i: (0, i, 0)),
            pl.BlockSpec((B_HEADS, tm, 256), lambda i: (0, i, 0)),
            pl.BlockSpec((B_HEADS, tm, 256), lambda i: (0, i, 0)),
        ],
        out_shape=[
            jax.ShapeDtypeStruct((R, 512), BF16),
            jax.ShapeDtypeStruct((R, LANES), BF16),
            jax.ShapeDtypeStruct((B_HEADS, R, 256), BF16),
            jax.ShapeDtypeStruct((B_HEADS, R, 256), BF16),
            jax.ShapeDtypeStruct((B_HEADS, R, 256), BF16),
        ],
        compiler_params=_cparams(("arbitrary",)),
        name="prep",
    )(pa, p1, pa, cos, sin, qn, wuq, kn, wukv)


def _swap_halves(t):
    return jnp.concatenate([t[:, 64:], t[:, :64]], axis=1)


def _win_kernel(sink_ref, q_ref, kp_ref, ks_ref, kn_ref, vp_ref, vs_ref, vn_ref, kc_ref, vc_ref, o_ref,
                *, L, R):
    i = pl.program_id(0)
    q = q_ref[...]
    kl = jnp.concatenate([kp_ref[...], ks_ref[...], kn_ref[...]], axis=0)
    vl = jnp.concatenate([vp_ref[...], vs_ref[...], vn_ref[...]], axis=0)
    kc = kc_ref[...]
    vc = vc_ref[...]
    lo = lax.broadcasted_iota(jnp.int32, (BLK, LANES), 1) < 64
    zero = jnp.zeros((BLK, LANES), BF16)

    rows = 4 * BLK
    rr = lax.broadcasted_iota(jnp.int32, (rows, 3 * BLK), 0)
    cc = lax.broadcasted_iota(jnp.int32, (rows, 3 * BLK), 1)
    qpos = (rr & (BLK - 1)) + i * BLK
    kpos = cc + (i - 1) * BLK
    ok = ((kpos >= L) & (kpos < R) & (qpos >= L)
          & (kpos - qpos <= WINDOW) & (qpos - kpos <= WINDOW))
    hrow = lax.broadcasted_iota(jnp.int32, (rows, 1), 0) // BLK

    for g in range(2):
        t0 = q[:, 256 * g:256 * g + 128]
        t1 = q[:, 256 * g + 128:256 * g + 256]
        if g == 0:
            parts = [jnp.where(lo, t0, zero), jnp.where(lo, _swap_halves(t0), zero),
                     jnp.where(lo, t1, zero), jnp.where(lo, _swap_halves(t1), zero)]
        else:
            parts = [jnp.where(lo, zero, _swap_halves(t0)), jnp.where(lo, zero, t0),
                     jnp.where(lo, zero, _swap_halves(t1)), jnp.where(lo, zero, t1)]
        qs = jnp.concatenate(parts, axis=0)
        s_l = jnp.where(ok, _dot_nt(qs, kl), -jnp.inf)
        s_c = _dot_nt(qs, kc)
        sink = jnp.zeros((rows, 1), F32)
        for j in range(4):
            sink = jnp.where(hrow == j, sink_ref[4 * g + j], sink)
        m = jnp.maximum(jnp.maximum(jnp.max(s_l, axis=-1, keepdims=True),
                                    jnp.max(s_c, axis=-1, keepdims=True)), sink)
        p_l = jnp.exp(s_l - m)
        p_c = jnp.exp(s_c - m)
        den = (jnp.sum(p_l, axis=-1, keepdims=True) + jnp.sum(p_c, axis=-1, keepdims=True)
               + jnp.exp(sink - m))
        o = (_dot(p_l.astype(BF16), vl) + _dot(p_c.astype(BF16), vc)) / den
        for pair in range(2):
            a = o[2 * pair * BLK:(2 * pair + 1) * BLK]
            b = o[(2 * pair + 1) * BLK:(2 * pair + 2) * BLK]
            if g == 0:
                tile = jnp.where(lo, a, _swap_halves(b))
            else:
                tile = jnp.where(lo, _swap_halves(a), b)
            c0 = 256 * g + 128 * pair
            o_ref[:, c0:c0 + 128] = tile.astype(BF16)


def _window_attn(sink, aq, ak, p, L):
    R = aq.shape[0]
    nb = R // BLK
    prev = lambda c: (lambda i: (jnp.maximum(i - 1, 0), c))
    this = lambda c: (lambda i: (i, c))
    nxt = lambda c: (lambda i: (jnp.minimum(i + 1, nb - 1), c))
    return pl.pallas_call(
        functools.partial(_win_kernel, L=L, R=R),
        grid=(nb,),
        in_specs=[
            pl.BlockSpec(memory_space=pltpu.SMEM),
            pl.BlockSpec((BLK, 512), lambda i: (i, 0)),
            pl.BlockSpec((BLK, LANES), prev(0)),
            pl.BlockSpec((BLK, LANES), this(0)),
            pl.BlockSpec((BLK, LANES), nxt(0)),
            pl.BlockSpec((BLK, LANES), prev(1)),
            pl.BlockSpec((BLK, LANES), this(1)),
            pl.BlockSpec((BLK, LANES), nxt(1)),
            pl.BlockSpec((L, LANES), lambda i: (0, 0)),
            pl.BlockSpec((L, LANES), lambda i: (0, 1)),
        ],
        out_specs=pl.BlockSpec((BLK, 512), lambda i: (i, 0)),
        out_shape=jax.ShapeDtypeStruct((R, 512), BF16),
        compiler_params=_cparams(("arbitrary",)),
        name="window_attn",
    )(sink, aq, ak, ak, ak, p, p, p, ak, p)


def _softmax_out(acc):
    return (acc[:, 0:B_VDIM] / acc[:, B_VDIM:2 * B_VDIM]).astype(BF16)


def _mla_kernel(ql_ref, qc_ref, k_ref, v_ref, ol_ref, oc_ref, s0_ref, s1_ref, m0_ref, m1_ref, *, L, tq, tk):
    i = pl.program_id(1)
    nk = k_ref.shape[0] // tk

    @pl.when(i == 0)
    def _():
        s1_ref[...] = jnp.zeros_like(s1_ref)
        m1_ref[...] = jnp.zeros_like(m1_ref)

    def step(sw_ref, mw_ref, sr_ref, mr_ref):
        q = ql_ref[...]
        m_prev = mr_ref[...]
        mrun = jnp.full((tq, LANES), -jnp.inf, F32)
        acc = jnp.zeros((tq, 2 * B_VDIM), F32)
        for c in range(nk):
            rows = slice(c * tk, (c + 1) * tk)
            s = _dot_nt(q, k_ref[rows, :])
            sw_ref[c] = s
            for t in range(tk // LANES):
                mrun = jnp.maximum(mrun, s[:, t * LANES:(t + 1) * LANES])
            sp = sr_ref[c]
            p = jnp.concatenate(
                [jnp.exp2(sp[:, t * LANES:(t + 1) * LANES] - m_prev) for t in range(tk // LANES)], axis=1)
            acc = acc + _dot(p.astype(BF16), v_ref[rows, :])
        mw_ref[...] = jnp.broadcast_to(jnp.max(mrun, axis=-1, keepdims=True), (tq, LANES))
        ol_ref[...] = _softmax_out(acc)

    @pl.when(i % 2 == 0)
    def _():
        step(s0_ref, m0_ref, s1_ref, m1_ref)

    @pl.when(i % 2 == 1)
    def _():
        step(s1_ref, m1_ref, s0_ref, m0_ref)

    @pl.when(i == 0)
    def _():
        s = _dot_nt(qc_ref[...], k_ref[0:L, :])
        p = jnp.exp2(s - jnp.max(s, axis=-1, keepdims=True))
        oc_ref[...] = _softmax_out(_dot(p.astype(BF16), v_ref[0:L, :]))


def _mla_attn(q, k, v, L):
    R = q.shape[1]
    T = R - L
    tq = _pick_tile(T, 512, 256)
    tk = 256
    nt = T // tq
    q_ctx = q[:, :L]
    q_lat = q[:, L:]
    once = pl.Buffered(1)
    o_lat, o_ctx = pl.pallas_call(
        functools.partial(_mla_kernel, L=L, tq=tq, tk=tk),
        grid=(B_HEADS, nt + 1),
        in_specs=[
            pl.BlockSpec((None, tq, 256), lambda h, i: (h, jnp.minimum(i, nt - 1), 0)),
            pl.BlockSpec((None, L, 256), lambda h, i: (h, 0, 0)),
            pl.BlockSpec((None, R, 256), lambda h, i: (h, 0, 0), pipeline_mode=once),
            pl.BlockSpec((None, R, 256), lambda h, i: (h, 0, 0), pipeline_mode=once),
        ],
        out_specs=[
            pl.BlockSpec((tq, B_VDIM), lambda h, i: (jnp.maximum(i - 1, 0), h)),
            pl.BlockSpec((L, B_VDIM), lambda h, i: (0, h)),
        ],
        out_shape=[
            jax.ShapeDtypeStruct((T, B_HEADS * B_VDIM), BF16),
            jax.ShapeDtypeStruct((L, B_HEADS * B_VDIM), BF16),
        ],
        scratch_shapes=[
            pltpu.VMEM((R // tk, tq, tk), F32),
            pltpu.VMEM((R // tk, tq, tk), F32),
            pltpu.VMEM((tq, LANES), F32),
            pltpu.VMEM((tq, LANES), F32),
        ],
        compiler_params=_cparams(("arbitrary", "arbitrary")),
        name="mla_attn",
    )(q_lat, q_ctx, k, v)
    return jnp.concatenate([o_ctx, o_lat], axis=0)


LEVELS = (64, 32, 16, 8, 4, 2, 1)


def _split3(x):
    hi = x.astype(BF16)
    r = x - hi.astype(F32)
    mid = r.astype(BF16)
    lo = (r - mid.astype(F32)).astype(BF16)
    return hi, mid, lo


def _hgrn_kernel(zf_ref, vf_ref, qf_ref, zb_ref, vb_ref, qb_ref, lb_ref, of_ref, ob_ref, st_ref):
    n = pl.program_id(0)

    @pl.when(n == 0)
    def _():
        st_ref[...] = jnp.zeros_like(st_ref)

    half = C_HEADS * C_DK
    width = 2 * half
    nh = 2 * C_HEADS
    heads = [slice(h * C_DK, (h + 1) * C_DK) for h in range(nh)]
    r0 = lax.broadcasted_iota(jnp.int32, (BLK, BLK), 0)
    c0 = lax.broadcasted_iota(jnp.int32, (BLK, BLK), 1)
    lt = (r0 >= c0).astype(BF16)
    rev = (r0 + c0 == BLK - 1).astype(BF16)
    ones = jnp.ones((BLK, BLK), BF16)
    rw = lax.broadcasted_iota(jnp.int32, (BLK, width), 0)
    sub8 = lax.broadcasted_iota(jnp.int32, (BLK // 8, 8, width), 1)

    z = jnp.concatenate([zf_ref[...].astype(F32), _dot(rev, zb_ref[...])], axis=1)
    v = jnp.concatenate([vf_ref[...].astype(F32), _dot(rev, vb_ref[...])], axis=1)
    vb = v.astype(BF16)
    qp = jnp.concatenate([qf_ref[...].astype(F32), _dot(rev, qb_ref[...])], axis=1)
    lb = jnp.concatenate([lb_ref[0], lb_ref[1]], axis=1)

    ls = jnp.minimum(z, 0.0) - jnp.log(1.0 + jnp.exp(-jnp.abs(z)))
    a1 = jnp.log(lb)
    a2 = jnp.log1p(-lb) + ls
    lf = jnp.maximum(a1, a2) + jnp.log(1.0 + jnp.exp(-jnp.abs(a1 - a2)))
    k = (1.0 - lb) * (1.0 / (1.0 + jnp.exp(z)))
    q = qp * _sigmoid(qp) * (C_DK ** -0.5)

    hi, mid, lo = _split3(lf * LOG2E)
    b = _dot(lt, hi) + _dot(lt, mid) + _dot(lt, lo)

    sts = [st_ref[h] for h in range(nh)]
    qe = (q * jnp.exp2(b)).astype(BF16)
    outs = [_dot_nt(qe[:, sl], st.astype(BF16)) for sl, st in zip(heads, sts)]

    b8 = b.reshape(BLK // 8, 8, width)
    atts = [jnp.zeros((BLK, BLK), F32) for _ in heads]
    for s in LEVELS:
        if s >= 8:
            g = BLK // (2 * s)
            e = jnp.broadcast_to(b.reshape(g, 2 * s, width)[:, s - 1:s, :], (g, 2 * s, width))
        elif s == 4:
            e = jnp.broadcast_to(b8[:, 3:4, :], b8.shape)
        elif s == 2:
            e = jnp.where(sub8 < 4, jnp.broadcast_to(b8[:, 1:2, :], b8.shape),
                          jnp.broadcast_to(b8[:, 5:6, :], b8.shape))
        else:
            e = jnp.where((sub8 & 1) == 0, b8, pltpu.roll(b8, 1, 1))
        w = jnp.exp2(-jnp.abs(b - e.reshape(BLK, width)))
        right = (rw & s) != 0
        qt = jnp.where(right, q * w, 0.0).astype(BF16)
        kt = jnp.where(right, 0.0, k * w).astype(BF16)
        pair = (r0 // (2 * s)) == (c0 // (2 * s))
        atts = [jnp.where(pair, _dot_nt(qt[:, sl], kt[:, sl]), att) for sl, att in zip(heads, atts)]
    dg = (q * k).astype(BF16)
    outs = [o + _dot(att.astype(BF16), vb[:, sl]) + _dot(dg[:, sl], ones) * v[:, sl]
            for sl, o, att in zip(heads, outs, atts)]

    bl = b[BLK - 1:BLK, :]
    kl = (k * jnp.exp2(bl - b)).astype(BF16)
    dec = jnp.exp2(bl)
    for h, sl in enumerate(heads):
        st_ref[h] = sts[h] * dec[:, sl] + _dot(v[:, sl].T.astype(BF16), kl[:, sl])

    of_ref[...] = jnp.concatenate(outs[:C_HEADS], axis=1)
    o = jnp.concatenate(outs[C_HEADS:], axis=1)
    ohi = o.astype(BF16)
    olo = (o - ohi.astype(F32)).astype(BF16)
    ob_ref[...] = _dot(rev, ohi) + _dot(rev, olo)


def _hgrn(pa, pc, lb, li, L):
    R = pa.shape[0]
    nb = R // BLK
    nc = L // BLK

    def back(n):
        return jnp.where(n < nc, nc - 1 - n, nb - 1 - (n - nc))

    fwd = lambda c: (lambda n: (n, c))
    bwd = lambda c: (lambda n: (back(n), c))
    return pl.pallas_call(
        _hgrn_kernel,
        grid=(nb,),
        in_specs=[
            pl.BlockSpec((BLK, 512), fwd(PA_CFF // 512)),
            pl.BlockSpec((BLK, 512), fwd(PA_CI // 512)),
            pl.BlockSpec((BLK, 512), fwd(PC_HQ // 512)),
            pl.BlockSpec((BLK, 512), bwd(PA_CFF // 512 + 1)),
            pl.BlockSpec((BLK, 512), bwd(PA_CI // 512)),
            pl.BlockSpec((BLK, 512), bwd(PC_HQ // 512)),
            pl.BlockSpec((None, 2, 1, 512), lambda n: (li, 0, 0, 0)),
        ],
        out_specs=[pl.BlockSpec((BLK, 512), fwd(0)), pl.BlockSpec((BLK, 512), bwd(0))],
        out_shape=[jax.ShapeDtypeStruct((R, 512), F32), jax.ShapeDtypeStruct((R, 512), F32)],
        scratch_shapes=[pltpu.VMEM((2 * C_HEADS, C_DK, C_DK), F32)],
        compiler_params=_cparams(("arbitrary",)),
        name="hgrn",
    )(pa, pa, pc, pa, pa, pc, lb)


POOL_TM = 256
POOL_HALO = 16


def _pool_kernel(up_ref, us_ref, un_ref, w_ref, sc_ref, o_ref, *, L, R):
    i = pl.program_id(0)
    tm, halo = POOL_TM, POOL_HALO
    u3 = jnp.concatenate([up_ref[...], us_ref[...], un_ref[...]], axis=0)
    us = us_ref[...].astype(F32)
    r = lax.broadcasted_iota(jnp.int32, (tm, tm + 2 * halo), 0) + i * tm
    s = lax.broadcasted_iota(jnp.int32, (tm, tm + 2 * halo), 1) + (i * tm - halo)
    seq_lo = jnp.where(r < L, 0, L)
    seq_hi = jnp.where(r < L, L, R)
    r1 = lax.broadcasted_iota(jnp.int32, (tm, LANES), 0) + i * tm
    seq_lo1 = jnp.where(r1 < L, 0, L)
    seq_hi1 = jnp.where(r1 < L, L, R)
    for g, w in enumerate(POOL_WINDOWS):
        sl = slice(g * LANES, (g + 1) * LANES)
        lo = jnp.maximum(r - w // 2, seq_lo)
        hi = jnp.minimum(r + w - w // 2, seq_hi)
        band = ((s >= lo) & (s < hi)).astype(BF16)
        cnt = (jnp.minimum(r1 + w - w // 2, seq_hi1) - jnp.maximum(r1 - w // 2, seq_lo1)).astype(F32)
        mean = _dot(band, u3[:, sl]) / cnt
        d = (mean - us[:, sl]).astype(BF16)
        o_ref[:, sl] = (_dot(d, w_ref[g]) * sc_ref[:, sl]).astype(BF16)


def _pool(pc, w_pool, pool_scale, li, L):
    R = pc.shape[0]
    tm, halo = POOL_TM, POOL_HALO
    assert max(POOL_WINDOWS) // 2 <= halo and tm % halo == 0
    nb = R // tm
    hb = tm // halo
    cb = PC_DX // 512
    return pl.pallas_call(
        functools.partial(_pool_kernel, L=L, R=R),
        grid=(nb,),
        in_specs=[
            pl.BlockSpec((halo, 512), lambda i: (jnp.maximum(i * hb - 1, 0), cb)),
            pl.BlockSpec((tm, 512), lambda i: (i, cb)),
            pl.BlockSpec((halo, 512), lambda i: (jnp.minimum((i + 1) * hb, R // halo - 1), cb)),
            pl.BlockSpec((None, len(POOL_WINDOWS), LANES, LANES), lambda i: (li, 0, 0, 0)),
            pl.BlockSpec((None, 1, 512), lambda i: (li, 0, 0)),
        ],
        out_specs=pl.BlockSpec((tm, 512), lambda i: (i, 0)),
        out_shape=jax.ShapeDtypeStruct((R, 512), BF16),
        compiler_params=_cparams(("arbitrary",)),
        name="pool",
    )(pc, pc, pc, w_pool, pool_scale)


def _merge_kernel(oa_ref, ob_ref, ohf_ref, ohb_ref, od_ref, gt_ref, mg0_ref, mg1_ref, mg2_ref, mg3_ref,
                  wbr_ref, wout_ref, hn_ref, x_ref, mod_ref, gpost_ref, gnext_ref, modn_ref,
                  xo_ref, ho_ref, *, ctx_tiles):
    is_ctx = pl.program_id(0) < ctx_tiles
    hs = ohf_ref[...] + ohb_ref[...]
    hn = hn_ref[...]
    oh = jnp.concatenate(
        [_rms(hs[:, h * C_DK:(h + 1) * C_DK], hn[:, h * C_DK:(h + 1) * C_DK]) for h in range(C_HEADS)], axis=1)
    outs = (oa_ref[...].astype(F32), ob_ref[...].astype(F32), oh, od_ref[...].astype(F32))
    mgs = (mg0_ref, mg1_ref, mg2_ref, mg3_ref)
    merged = None
    for nbr in range(N_BRANCH):
        gate = gt_ref[:, nbr * BRANCH_W:(nbr + 1) * BRANCH_W].astype(F32)
        ys = (outs[nbr] * (gate * _sigmoid(gate))).astype(BF16)
        yb = _dot(ys, wbr_ref[nbr])
        term = _sigmoid(mgs[nbr][...].astype(F32)) * yb
        merged = term if merged is None else merged + term
    y = _dot(merged.astype(BF16), wout_ref[...])
    _, _, gate_r = _mod_rows(mod_ref, is_ctx)
    xn = x_ref[...] + gate_r * _rms(y, gpost_ref[...])
    xo_ref[...] = xn
    shift_n, scale_n, _ = _mod_rows(modn_ref, is_ctx)
    ho_ref[...] = (_rms(xn, gnext_ref[...]) * (1.0 + scale_n) + shift_n).astype(BF16)


def _merge(oa, ob, ohf, ohb, od, pb, wbr, wout, hnorm, xs, mod, g_post, g_pre, li, L):
    R = xs.shape[0]
    tm = 256
    ln = min(li + 1, DEPTH - 1)
    row = lambda i: (i, 0)
    gcol = 0
    mcol = 1
    return pl.pallas_call(
        functools.partial(_merge_kernel, ctx_tiles=L // tm),
        grid=(R // tm,),
        in_specs=[
            pl.BlockSpec((tm, 512), row),
            pl.BlockSpec((tm, 512), row),
            pl.BlockSpec((tm, 512), row),
            pl.BlockSpec((tm, 512), row),
            pl.BlockSpec((tm, 512), row),
            pl.BlockSpec((tm, D_MODEL), lambda i: (i, gcol)),
            pl.BlockSpec((tm, D_MODEL), lambda i: (i, mcol)),
            pl.BlockSpec((tm, D_MODEL), lambda i: (i, mcol + 1)),
            pl.BlockSpec((tm, D_MODEL), lambda i: (i, mcol + 2)),
            pl.BlockSpec((tm, D_MODEL), lambda i: (i, mcol + 3)),
            pl.BlockSpec((None, N_BRANCH, BRANCH_W, D_MODEL), lambda i: (li, 0, 0, 0)),
            pl.BlockSpec((None, D_MODEL, D_MODEL), lambda i: (li, 0, 0)),
            pl.BlockSpec((None, 1, 512), lambda i: (li, 0, 0)),
            pl.BlockSpec((tm, D_MODEL), row),
            pl.BlockSpec((None, 8, 3 * D_MODEL), lambda i: (li, 0, 0)),
            pl.BlockSpec((None, 1, D_MODEL), lambda i: (li, 0, 0)),
            pl.BlockSpec((None, 1, D_MODEL), lambda i: (ln, 0, 0)),
            pl.BlockSpec((None, 8, 3 * D_MODEL), lambda i: (ln, 0, 0)),
        ],
        out_specs=[pl.BlockSpec((tm, D_MODEL), row), pl.BlockSpec((tm, D_MODEL), row)],
        out_shape=[jax.ShapeDtypeStruct((R, D_MODEL), F32), jax.ShapeDtypeStruct((R, D_MODEL), BF16)],
        compiler_params=_cparams(("arbitrary",)),
        name="merge",
    )(oa, ob, ohf, ohb, od, pb, pb, pb, pb, pb, wbr, wout, hnorm, xs, mod, g_post, g_pre, mod)


def _rope_tables(T, L):
    rows = jnp.repeat(jnp.arange(T // GRID_W, dtype=jnp.int32), GRID_W)
    cols = jnp.tile(jnp.arange(GRID_W, dtype=jnp.int32), T // GRID_W)
    half = 16
    freqs = ROPE_BASE ** (-jnp.arange(half, dtype=F32) / half)
    ar = rows.astype(F32)[:, None] * freqs[None, :]
    ac = cols.astype(F32)[:, None] * freqs[None, :]
    cos = jnp.concatenate([jnp.cos(ar), jnp.cos(ar), jnp.cos(ac), jnp.cos(ac)], axis=1)
    sin = jnp.concatenate([-jnp.sin(ar), jnp.sin(ar), -jnp.sin(ac), jnp.sin(ac)], axis=1)
    cos = jnp.concatenate([jnp.ones((L, 64), F32), cos], axis=0)
    sin = jnp.concatenate([jnp.zeros((L, 64), F32), sin], axis=0)
    return jnp.tile(cos, (1, 2)), jnp.tile(sin, (1, 2))


def kernel(x, c, ctx, c_ctx, w_ada, b_ada, g_pre, g_post, w_in, a_sink, mla_q_norm, w_uq, mla_kv_norm,
           w_ukv, hgrn_lb, hgrn_norm, w_pool, pool_scale, w_branch, w_out):
    assert x.shape[0] == 1 and ctx.shape[0] == 1
    T = x.shape[1]
    L = ctx.shape[1]
    assert T % 256 == 0 and L % 256 == 0 and T % GRID_W == 0

    w_in = w_in.astype(F32)
    wuq = w_uq.astype(BF16).reshape(DEPTH, B_Q_LORA, B_HEADS, B_NOPE + B_ROPE)
    wuq = jnp.pad(wuq, ((0, 0), (0, 512 - B_Q_LORA), (0, 0), (0, 256 - B_NOPE - B_ROPE))).reshape(DEPTH, 512, 1024)
    qn = jnp.pad(mla_q_norm.astype(F32), ((0, 0), (0, 512 - B_Q_LORA))).reshape(DEPTH, 1, 512)
    kn = mla_kv_norm.astype(F32).reshape(DEPTH, 1, B_KV_LORA)
    wukv = w_ukv.astype(BF16)
    wpool = w_pool.astype(BF16)
    pscale = pool_scale.astype(F32).reshape(DEPTH, 1, BRANCH_W)
    wbr = w_branch.astype(BF16)
    wout = w_out.astype(BF16)
    hnorm = hgrn_norm.astype(F32).reshape(DEPTH, 1, C_HEADS * C_DK)
    gpre = g_pre.astype(F32).reshape(DEPTH, 1, D_MODEL)
    gpost = g_post.astype(F32).reshape(DEPTH, 1, D_MODEL)

    cos, sin = _rope_tables(T, L)

    cc = jnp.concatenate([c.astype(F32), c_ctx.astype(F32)[None, :], jnp.zeros((6, D_MODEL), F32)], axis=0)
    mod = _ada_all(cc, w_ada, b_ada)
    lb = _lower_bounds(hgrn_lb)

    xs = jnp.concatenate([ctx[0], x[0]], axis=0).astype(F32)
    h = _prenorm(xs, gpre, mod, 0, L)
    for li in range(DEPTH):
        p1 = _proj(h, w_in, li, 0, O_P1_END, 512)
        pa = _proj(h, w_in, li, O_PA, PA_WIDTH, 512)
        pc = _proj(h, w_in, li, O_PC, O_PB - O_PC, 1024)
        pb = _proj(h, w_in, li, O_PB, O_END - O_PB, 1024)
        aq, ak, mq, mk, mv = _prep(p1, pa, cos, sin, qn, wuq, kn, wukv, li)
        oa = _window_attn(a_sink[li].astype(F32), aq, ak, p1, L)
        ob = _mla_attn(mq, mk, mv, L)
        ohf, ohb = _hgrn(pa, pc, lb, li, L)
        od = _pool(pc, wpool, pscale, li, L)
        xs, h = _merge(oa, ob, ohf, ohb, od, pb, wbr, wout, hnorm, xs, mod, gpost, gpre, li, L)
    return xs[L:][None].astype(x.dtype)
```

```python
import functools

import jax
import jax.numpy as jnp
from jax import lax
from jax.experimental import pallas as pl
from jax.experimental.pallas import tpu as pltpu

F32 = jnp.float32
BF16 = jnp.bfloat16

D_MODEL = 2048
DEPTH = 4
GRID_W = 64
EPS = 1e-6
ROPE_BASE = 10000.0
N_BRANCH = 4
BRANCH_W = 512
A_HEADS = 8
A_HEAD_DIM = 64
WINDOW = 128
B_HEADS = 4
B_NOPE = 128
B_ROPE = 64
B_VDIM = 128
B_Q_LORA = 384
B_KV_LORA = 128
C_HEADS = 4
C_DK = 128
POOL_WINDOWS = (2, 4, 8, 16)

O_P1_END = 512
O_PA = 448
PA_WIDTH = 2560
O_PC = 2880
O_PB = 3904
O_END = 14144
PA_CFF = 0
PA_CI = 1024
PA_AQ = 1536
PA_CQ = 2048
PC_HQ = 0
PC_DX = 512

BLK = 128
LANES = 128
VMEM_LIMIT = 56 * 1024 * 1024

NT_DIMS = (((1,), (1,)), ((), ()))
LOG2E = 1.4426950408889634


def _cparams(sem):
    return pltpu.CompilerParams(dimension_semantics=sem, vmem_limit_bytes=VMEM_LIMIT)


def _pick_tile(n, cap, mult=BLK):
    best = mult
    t = mult
    while t <= min(n, cap):
        if n % t == 0:
            best = t
        t += mult
    return best


def _dot(a, b):
    return jnp.dot(a, b, preferred_element_type=F32)


def _dot_nt(a, b):
    return lax.dot_general(a, b, NT_DIMS, preferred_element_type=F32)


def _sigmoid(x):
    return 0.5 * jnp.tanh(0.5 * x) + 0.5


def _ada_kernel(cc_ref, w_ref, b_ref, o_ref):
    cc = cc_ref[...]
    s = (cc * _sigmoid(cc)).astype(BF16)
    o_ref[...] = _dot(s, w_ref[...].astype(BF16)) + b_ref[...]


def _ada_all(cc, w_ada, b_ada):
    tn = 1024
    n3 = 3 * D_MODEL
    return pl.pallas_call(
        _ada_kernel,
        grid=(DEPTH, n3 // tn),
        in_specs=[
            pl.BlockSpec((8, D_MODEL), lambda l, j: (0, 0)),
            pl.BlockSpec((None, D_MODEL, tn), lambda l, j: (l, 0, j)),
            pl.BlockSpec((None, 1, tn), lambda l, j: (l, 0, j)),
        ],
        out_specs=pl.BlockSpec((None, 8, tn), lambda l, j: (l, 0, j)),
        out_shape=jax.ShapeDtypeStruct((DEPTH, 8, n3), F32),
        compiler_params=_cparams(("arbitrary", "arbitrary")),
        name="adaln",
    )(cc, w_ada, b_ada.reshape(DEPTH, 1, n3))


def _lb_kernel(x_ref, o_ref):
    x = x_ref[...]
    m = jnp.max(x, axis=0, keepdims=True)
    e = jnp.exp(x - m)
    sm = e / jnp.sum(e, axis=0, keepdims=True)
    rows = []
    run = sm[0:1]
    first = run
    for l in range(DEPTH):
        if l > 0:
            run = run + sm[l:l + 1]
        rows.append(run - first)
    o_ref[...] = jnp.concatenate(rows, axis=0)


def _lower_bounds(hgrn_lb):
    n = 2 * C_HEADS * C_DK
    out = pl.pallas_call(
        _lb_kernel,
        out_shape=jax.ShapeDtypeStruct((DEPTH, n), F32),
        name="hgrn_lb",
    )(hgrn_lb.reshape(DEPTH, n).astype(F32))
    return out.reshape(DEPTH, 2, 1, C_HEADS * C_DK)


def _rms(x, g):
    return x * lax.rsqrt(jnp.mean(x * x, axis=-1, keepdims=True) + EPS) * g


def _mod_rows(mod_ref, is_ctx):
    m = mod_ref[...]
    sel = jnp.where(is_ctx, m[1:2, :], m[0:1, :])
    return sel[:, 0:D_MODEL], sel[:, D_MODEL:2 * D_MODEL], sel[:, 2 * D_MODEL:3 * D_MODEL]


def _prenorm_kernel(x_ref, g_ref, mod_ref, h_ref, *, ctx_tiles):
    is_ctx = pl.program_id(0) < ctx_tiles
    shift, scale, _ = _mod_rows(mod_ref, is_ctx)
    h = _rms(x_ref[...], g_ref[...]) * (1.0 + scale) + shift
    h_ref[...] = h.astype(BF16)


def _prenorm(xs, g_pre, mod, li, L):
    R = xs.shape[0]
    tm = 256
    return pl.pallas_call(
        functools.partial(_prenorm_kernel, ctx_tiles=L // tm),
        grid=(R // tm,),
        in_specs=[
            pl.BlockSpec((tm, D_MODEL), lambda i: (i, 0)),
            pl.BlockSpec((None, 1, D_MODEL), lambda i: (li, 0, 0)),
            pl.BlockSpec((None, 8, 3 * D_MODEL), lambda i: (li, 0, 0)),
        ],
        out_specs=pl.BlockSpec((tm, D_MODEL), lambda i: (i, 0)),
        out_shape=jax.ShapeDtypeStruct((R, D_MODEL), BF16),
        compiler_params=_cparams(("arbitrary",)),
        name="prenorm",
    )(xs, g_pre, mod)


def _proj_kernel(h_ref, wt_ref, o_ref, wbf_ref):
    @pl.when(pl.program_id(1) == 0)
    def _():
        wbf_ref[...] = wt_ref[...].astype(BF16)

    o_ref[...] = _dot_nt(h_ref[...], wbf_ref[...]).astype(o_ref.dtype)


def _proj(h, w_t, li, start, width, tn):
    R = h.shape[0]
    tm = _pick_tile(R, 1408)
    assert width % tn == 0 and start % 8 == 0 and start + width <= w_t.shape[1]
    return pl.pallas_call(
        _proj_kernel,
        grid=(width // tn, R // tm),
        in_specs=[
            pl.BlockSpec((tm, D_MODEL), lambda j, i: (i, 0)),
            pl.BlockSpec((None, pl.Element(tn), pl.Element(D_MODEL)),
                         lambda j, i: (li, pl.multiple_of(start + j * tn, 8), 0)),
        ],
        out_specs=pl.BlockSpec((tm, tn), lambda j, i: (i, j)),
        out_shape=jax.ShapeDtypeStruct((R, width), BF16),
        scratch_shapes=[pltpu.VMEM((tn, D_MODEL), BF16)],
        compiler_params=_cparams(("arbitrary", "arbitrary")),
        name="proj",
    )(h, w_t)


def _rope_tile(t, cos, sin):
    lane = lax.broadcasted_iota(jnp.int32, t.shape, 1)
    first = (lane & 16) == 0
    partner = jnp.where(first, pltpu.roll(t, LANES - 16, 1), pltpu.roll(t, 16, 1))
    return t * cos + partner * sin


def _prep_kernel(aq_ref, kv_ref, cq_ref, cos_ref, sin_ref, qn_ref, wuq_ref, kn_ref, wukv_ref,
                 aq_o, ak_o, q_o, k_o, v_o):
    cos = cos_ref[...]
    sin = sin_ref[...]
    a_scale = A_HEAD_DIM ** -0.5
    for t in range(A_HEADS * A_HEAD_DIM // LANES):
        sl = slice(t * LANES, (t + 1) * LANES)
        aq_o[:, sl] = (_rope_tile(aq_ref[:, sl].astype(F32), cos, sin) * a_scale).astype(BF16)
    ak_o[...] = _rope_tile(kv_ref[:, 0:128].astype(F32), cos, sin).astype(BF16)

    ckv = kv_ref[:, 256:384].astype(F32)
    ckv = _rms(ckv, kn_ref[...])
    kv = _dot(ckv.astype(BF16), wukv_ref[...])
    krt = kv_ref[:, 384:512].astype(F32)
    krt = jnp.where(lax.broadcasted_iota(jnp.int32, krt.shape, 1) < B_ROPE, krt, 0.0)
    kr = _rope_tile(krt, cos, sin).astype(BF16)
    for h in range(B_HEADS):
        k_o[h, :, 0:128] = kv[:, 256 * h:256 * h + 128].astype(BF16)
        k_o[h, :, 128:256] = kr
        v_o[h, :, 0:128] = kv[:, 256 * h + 128:256 * h + 256].astype(BF16)
        v_o[h, :, 128:256] = jnp.ones((kv.shape[0], 128), BF16)

    cq = cq_ref[...].astype(F32)
    cq = jnp.where(lax.broadcasted_iota(jnp.int32, cq.shape, 1) < B_Q_LORA, cq, 0.0)
    cq = cq * lax.rsqrt(jnp.sum(cq * cq, axis=-1, keepdims=True) * (1.0 / B_Q_LORA) + EPS) * qn_ref[...]
    q = _dot(cq.astype(BF16), wuq_ref[...])
    b_scale = (B_NOPE + B_ROPE) ** -0.5 * LOG2E
    for h in range(B_HEADS):
        q_o[h, :, 0:128] = (q[:, 256 * h:256 * h + 128] * b_scale).astype(BF16)
        q_o[h, :, 128:256] = (_rope_tile(q[:, 256 * h + 128:256 * h + 256], cos, sin) * b_scale).astype(BF16)


def _prep(p1, pa, cos, sin, qn, wuq, kn, wukv, li):
    R = p1.shape[0]
    tm = 256
    row = lambda i: (i, 0)
    return pl.pallas_call(
        _prep_kernel,
        grid=(R // tm,),
        in_specs=[
            pl.BlockSpec((tm, 512), lambda i: (i, PA_AQ // 512)),
            pl.BlockSpec((tm, 512), row),
            pl.BlockSpec((tm, 512), lambda i: (i, PA_CQ // 512)),
            pl.BlockSpec((tm, LANES), row),
            pl.BlockSpec((tm, LANES), row),
            pl.BlockSpec((None, 1, 512), lambda i: (li, 0, 0)),
            pl.BlockSpec((None, 512, 1024), lambda i: (li, 0, 0)),
            pl.BlockSpec((None, 1, B_KV_LORA), lambda i: (li, 0, 0)),
            pl.BlockSpec((None, B_KV_LORA, 1024), lambda i: (li, 0, 0)),
        ],
        out_specs=[
            pl.BlockSpec((tm, 512), row),
            pl.BlockSpec((tm, LANES), row),
            pl.BlockSpec((B_HEADS, tm, 256), lambda i: (0, i, 0)),
            pl.BlockSpec((B_HEADS, tm, 256), lambda i: (0, i, 0)),
            pl.BlockSpec((B_HEADS, tm, 256), lambda i: (0, i, 0)),
        ],
        out_shape=[
            jax.ShapeDtypeStruct((R, 512), BF16),
            jax.ShapeDtypeStruct((R, LANES), BF16),
            jax.ShapeDtypeStruct((B_HEADS, R, 256), BF16),
            jax.ShapeDtypeStruct((B_HEADS, R, 256), BF16),
            jax.ShapeDtypeStruct((B_HEADS, R, 256), BF16),
        ],
        compiler_params=_cparams(("arbitrary",)),
        name="prep",
    )(pa, p1, pa, cos, sin, qn, wuq, kn, wukv)


def _swap_halves(t):
    return jnp.concatenate([t[:, 64:], t[:, :64]], axis=1)


WIN_TM = 2 * BLK


def _win_kernel(sink_ref, q_ref, kp_ref, ks_ref, kn_ref, vp_ref, vs_ref, vn_ref, kc_ref, vc_ref, o_ref,
                *, L, R):
    i = pl.program_id(0)
    k_all = jnp.concatenate([kp_ref[...], ks_ref[...], kn_ref[...]], axis=0)
    v_all = jnp.concatenate([vp_ref[...], vs_ref[...], vn_ref[...]], axis=0)
    kc = kc_ref[...]
    vc = vc_ref[...]
    lo = lax.broadcasted_iota(jnp.int32, (BLK, LANES), 1) < 64
    zero = jnp.zeros((BLK, LANES), BF16)
    rows = 4 * BLK
    rr = lax.broadcasted_iota(jnp.int32, (rows, 3 * BLK), 0)
    cc = lax.broadcasted_iota(jnp.int32, (rows, 3 * BLK), 1)
    hrow = lax.broadcasted_iota(jnp.int32, (rows, 1), 0) // BLK

    for sb in range(WIN_TM // BLK):
        bi = i * (WIN_TM // BLK) + sb
        q = q_ref[sb * BLK:(sb + 1) * BLK, :]
        kl = k_all[sb * BLK:(sb + 3) * BLK]
        vl = v_all[sb * BLK:(sb + 3) * BLK]
        qpos = (rr & (BLK - 1)) + bi * BLK
        kpos = cc + (bi - 1) * BLK
        ok = ((kpos >= L) & (kpos < R) & (qpos >= L)
              & (kpos - qpos <= WINDOW) & (qpos - kpos <= WINDOW))
        for g in range(2):
            t0 = q[:, 256 * g:256 * g + 128]
            t1 = q[:, 256 * g + 128:256 * g + 256]
            if g == 0:
                parts = [jnp.where(lo, t0, zero), jnp.where(lo, _swap_halves(t0), zero),
                         jnp.where(lo, t1, zero), jnp.where(lo, _swap_halves(t1), zero)]
            else:
                parts = [jnp.where(lo, zero, _swap_halves(t0)), jnp.where(lo, zero, t0),
                         jnp.where(lo, zero, _swap_halves(t1)), jnp.where(lo, zero, t1)]
            qs = jnp.concatenate(parts, axis=0)
            s_l = jnp.where(ok, _dot_nt(qs, kl), -jnp.inf)
            s_c = _dot_nt(qs, kc)
            sink = jnp.zeros((rows, 1), F32)
            for j in range(4):
                sink = jnp.where(hrow == j, sink_ref[4 * g + j], sink)
            m = jnp.maximum(jnp.maximum(jnp.max(s_l, axis=-1, keepdims=True),
                                        jnp.max(s_c, axis=-1, keepdims=True)), sink)
            p_l = jnp.exp(s_l - m)
            p_c = jnp.exp(s_c - m)
            den = (jnp.sum(p_l, axis=-1, keepdims=True) + jnp.sum(p_c, axis=-1, keepdims=True)
                   + jnp.exp(sink - m))
            o = (_dot(p_l.astype(BF16), vl) + _dot(p_c.astype(BF16), vc)) / den
            for pair in range(2):
                a = o[2 * pair * BLK:(2 * pair + 1) * BLK]
                b = o[(2 * pair + 1) * BLK:(2 * pair + 2) * BLK]
                if g == 0:
                    tile = jnp.where(lo, a, _swap_halves(b))
                else:
                    tile = jnp.where(lo, _swap_halves(a), b)
                c0 = 256 * g + 128 * pair
                o_ref[sb * BLK:(sb + 1) * BLK, c0:c0 + 128] = tile.astype(BF16)


def _window_attn(sink, aq, ak, p, L):
    R = aq.shape[0]
    nb = R // BLK
    per = WIN_TM // BLK
    prev = lambda c: (lambda i: (jnp.maximum(i * per - 1, 0), c))
    this = lambda c: (lambda i: (i, c))
    nxt = lambda c: (lambda i: (jnp.minimum((i + 1) * per, nb - 1), c))
    return pl.pallas_call(
        functools.partial(_win_kernel, L=L, R=R),
        grid=(R // WIN_TM,),
        in_specs=[
            pl.BlockSpec(memory_space=pltpu.SMEM),
            pl.BlockSpec((WIN_TM, 512), lambda i: (i, 0)),
            pl.BlockSpec((BLK, LANES), prev(0)),
            pl.BlockSpec((WIN_TM, LANES), this(0)),
            pl.BlockSpec((BLK, LANES), nxt(0)),
            pl.BlockSpec((BLK, LANES), prev(1)),
            pl.BlockSpec((WIN_TM, LANES), this(1)),
            pl.BlockSpec((BLK, LANES), nxt(1)),
            pl.BlockSpec((L, LANES), lambda i: (0, 0)),
            pl.BlockSpec((L, LANES), lambda i: (0, 1)),
        ],
        out_specs=pl.BlockSpec((WIN_TM, 512), lambda i: (i, 0)),
        out_shape=jax.ShapeDtypeStruct((R, 512), BF16),
        compiler_params=_cparams(("arbitrary",)),
        name="window_attn",
    )(sink, aq, ak, ak, ak, p, p, p, ak, p)


def _softmax_out(acc):
    return (acc[:, 0:B_VDIM] / acc[:, B_VDIM:2 * B_VDIM]).astype(BF16)


def _mla_kernel(ql_ref, qc_ref, k_ref, v_ref, ol_ref, oc_ref, s0_ref, s1_ref, m0_ref, m1_ref, *, L, tq, tk):
    i = pl.program_id(1)
    nk = k_ref.shape[0] // tk

    @pl.when(i == 0)
    def _():
        s1_ref[...] = jnp.zeros_like(s1_ref)
        m1_ref[...] = jnp.zeros_like(m1_ref)

    def step(sw_ref, mw_ref, sr_ref, mr_ref):
        q = ql_ref[...]
        m_prev = mr_ref[...]
        mrun = jnp.full((tq, LANES), -jnp.inf, F32)
        acc = jnp.zeros((tq, 2 * B_VDIM), F32)
        for c in range(nk):
            rows = slice(c * tk, (c + 1) * tk)
            s = _dot_nt(q, k_ref[rows, :])
            sw_ref[c] = s
            for t in range(tk // LANES):
                mrun = jnp.maximum(mrun, s[:, t * LANES:(t + 1) * LANES])
            sp = sr_ref[c]
            p = jnp.concatenate(
                [jnp.exp2(sp[:, t * LANES:(t + 1) * LANES] - m_prev) for t in range(tk // LANES)], axis=1)
            acc = acc + _dot(p.astype(BF16), v_ref[rows, :])
        mw_ref[...] = jnp.broadcast_to(jnp.max(mrun, axis=-1, keepdims=True), (tq, LANES))
        ol_ref[...] = _softmax_out(acc)

    @pl.when(i % 2 == 0)
    def _():
        step(s0_ref, m0_ref, s1_ref, m1_ref)

    @pl.when(i % 2 == 1)
    def _():
        step(s1_ref, m1_ref, s0_ref, m0_ref)

    @pl.when(i == 0)
    def _():
        s = _dot_nt(qc_ref[...], k_ref[0:L, :])
        p = jnp.exp2(s - jnp.max(s, axis=-1, keepdims=True))
        oc_ref[...] = _softmax_out(_dot(p.astype(BF16), v_ref[0:L, :]))


def _mla_attn(q, k, v, L):
    R = q.shape[1]
    T = R - L
    tq = _pick_tile(T, 512, 256)
    tk = 256
    nt = T // tq
    q_ctx = q[:, :L]
    q_lat = q[:, L:]
    once = pl.Buffered(1)
    o_lat, o_ctx = pl.pallas_call(
        functools.partial(_mla_kernel, L=L, tq=tq, tk=tk),
        grid=(B_HEADS, nt + 1),
        in_specs=[
            pl.BlockSpec((None, tq, 256), lambda h, i: (h, jnp.minimum(i, nt - 1), 0)),
            pl.BlockSpec((None, L, 256), lambda h, i: (h, 0, 0)),
            pl.BlockSpec((None, R, 256), lambda h, i: (h, 0, 0), pipeline_mode=once),
            pl.BlockSpec((None, R, 256), lambda h, i: (h, 0, 0), pipeline_mode=once),
        ],
        out_specs=[
            pl.BlockSpec((tq, B_VDIM), lambda h, i: (jnp.maximum(i - 1, 0), h)),
            pl.BlockSpec((L, B_VDIM), lambda h, i: (0, h)),
        ],
        out_shape=[
            jax.ShapeDtypeStruct((T, B_HEADS * B_VDIM), BF16),
            jax.ShapeDtypeStruct((L, B_HEADS * B_VDIM), BF16),
        ],
        scratch_shapes=[
            pltpu.VMEM((R // tk, tq, tk), F32),
            pltpu.VMEM((R // tk, tq, tk), F32),
            pltpu.VMEM((tq, LANES), F32),
            pltpu.VMEM((tq, LANES), F32),
        ],
        compiler_params=_cparams(("arbitrary", "arbitrary")),
        name="mla_attn",
    )(q_lat, q_ctx, k, v)
    return jnp.concatenate([o_ctx, o_lat], axis=0)


LEVELS = (64, 32, 16, 8, 4, 2, 1)


def _split3(x):
    hi = x.astype(BF16)
    r = x - hi.astype(F32)
    mid = r.astype(BF16)
    lo = (r - mid.astype(F32)).astype(BF16)
    return hi, mid, lo


def _hgrn_kernel(zf_ref, vf_ref, qf_ref, zb_ref, vb_ref, qb_ref, lb_ref, of_ref, ob_ref, st_ref):
    n = pl.program_id(0)

    @pl.when(n == 0)
    def _():
        st_ref[...] = jnp.zeros_like(st_ref)

    half = C_HEADS * C_DK
    width = 2 * half
    nh = 2 * C_HEADS
    heads = [slice(h * C_DK, (h + 1) * C_DK) for h in range(nh)]
    r0 = lax.broadcasted_iota(jnp.int32, (BLK, BLK), 0)
    c0 = lax.broadcasted_iota(jnp.int32, (BLK, BLK), 1)
    lt = (r0 >= c0).astype(BF16)
    rev = (r0 + c0 == BLK - 1).astype(BF16)
    ones = jnp.ones((BLK, BLK), BF16)
    rw = lax.broadcasted_iota(jnp.int32, (BLK, width), 0)
    sub8 = lax.broadcasted_iota(jnp.int32, (BLK // 8, 8, width), 1)

    z = jnp.concatenate([zf_ref[...].astype(F32), _dot(rev, zb_ref[...])], axis=1)
    v = jnp.concatenate([vf_ref[...].astype(F32), _dot(rev, vb_ref[...])], axis=1)
    vb = v.astype(BF16)
    qp = jnp.concatenate([qf_ref[...].astype(F32), _dot(rev, qb_ref[...])], axis=1)
    lb = jnp.concatenate([lb_ref[0], lb_ref[1]], axis=1)

    ls = jnp.minimum(z, 0.0) - jnp.log(1.0 + jnp.exp(-jnp.abs(z)))
    a1 = jnp.log(lb)
    a2 = jnp.log1p(-lb) + ls
    lf = jnp.maximum(a1, a2) + jnp.log(1.0 + jnp.exp(-jnp.abs(a1 - a2)))
    k = (1.0 - lb) * (1.0 / (1.0 + jnp.exp(z)))
    q = qp * _sigmoid(qp) * (C_DK ** -0.5)

    hi, mid, lo = _split3(lf * LOG2E)
    b = _dot(lt, hi) + _dot(lt, mid) + _dot(lt, lo)

    sts = [st_ref[h] for h in range(nh)]
    qe = (q * jnp.exp2(b)).astype(BF16)
    outs = [_dot_nt(qe[:, sl], st.astype(BF16)) for sl, st in zip(heads, sts)]

    b8 = b.reshape(BLK // 8, 8, width)
    atts = [jnp.zeros((BLK, BLK), F32) for _ in heads]
    for s in LEVELS:
        if s >= 8:
            g = BLK // (2 * s)
            e = jnp.broadcast_to(b.reshape(g, 2 * s, width)[:, s - 1:s, :], (g, 2 * s, width))
        elif s == 4:
            e = jnp.broadcast_to(b8[:, 3:4, :], b8.shape)
        elif s == 2:
            e = jnp.where(sub8 < 4, jnp.broadcast_to(b8[:, 1:2, :], b8.shape),
                          jnp.broadcast_to(b8[:, 5:6, :], b8.shape))
        else:
            e = jnp.where((sub8 & 1) == 0, b8, pltpu.roll(b8, 1, 1))
        w = jnp.exp2(-jnp.abs(b - e.reshape(BLK, width)))
        right = (rw & s) != 0
        qt = jnp.where(right, q * w, 0.0).astype(BF16)
        kt = jnp.where(right, 0.0, k * w).astype(BF16)
        pair = (r0 // (2 * s)) == (c0 // (2 * s))
        atts = [jnp.where(pair, _dot_nt(qt[:, sl], kt[:, sl]), att) for sl, att in zip(heads, atts)]
    dg = (q * k).astype(BF16)
    outs = [o + _dot(att.astype(BF16), vb[:, sl]) + _dot(dg[:, sl], ones) * v[:, sl]
            for sl, o, att in zip(heads, outs, atts)]

    bl = b[BLK - 1:BLK, :]
    kl = (k * jnp.exp2(bl - b)).astype(BF16)
    dec = jnp.exp2(bl)
    for h, sl in enumerate(heads):
        st_ref[h] = sts[h] * dec[:, sl] + _dot(v[:, sl].T.astype(BF16), kl[:, sl])

    of_ref[...] = jnp.concatenate(outs[:C_HEADS], axis=1)
    o = jnp.concatenate(outs[C_HEADS:], axis=1)
    ohi = o.astype(BF16)
    olo = (o - ohi.astype(F32)).astype(BF16)
    ob_ref[...] = _dot(rev, ohi) + _dot(rev, olo)


def _hgrn(pa, pc, lb, li, L):
    R = pa.shape[0]
    nb = R // BLK
    nc = L // BLK

    def back(n):
        return jnp.where(n < nc, nc - 1 - n, nb - 1 - (n - nc))

    fwd = lambda c: (lambda n: (n, c))
    bwd = lambda c: (lambda n: (back(n), c))
    return pl.pallas_call(
        _hgrn_kernel,
        grid=(nb,),
        in_specs=[
            pl.BlockSpec((BLK, 512), fwd(PA_CFF // 512)),
            pl.BlockSpec((BLK, 512), fwd(PA_CI // 512)),
            pl.BlockSpec((BLK, 512), fwd(PC_HQ // 512)),
            pl.BlockSpec((BLK, 512), bwd(PA_CFF // 512 + 1)),
            pl.BlockSpec((BLK, 512), bwd(PA_CI // 512)),
            pl.BlockSpec((BLK, 512), bwd(PC_HQ // 512)),
            pl.BlockSpec((None, 2, 1, 512), lambda n: (li, 0, 0, 0)),
        ],
        out_specs=[pl.BlockSpec((BLK, 512), fwd(0)), pl.BlockSpec((BLK, 512), bwd(0))],
        out_shape=[jax.ShapeDtypeStruct((R, 512), F32), jax.ShapeDtypeStruct((R, 512), F32)],
        scratch_shapes=[pltpu.VMEM((2 * C_HEADS, C_DK, C_DK), F32)],
        compiler_params=_cparams(("arbitrary",)),
        name="hgrn",
    )(pa, pa, pc, pa, pa, pc, lb)


POOL_TM = 256
POOL_HALO = 16


def _pool_kernel(up_ref, us_ref, un_ref, w_ref, sc_ref, o_ref, *, L, R):
    i = pl.program_id(0)
    tm, halo = POOL_TM, POOL_HALO
    u3 = jnp.concatenate([up_ref[...], us_ref[...], un_ref[...]], axis=0)
    us = us_ref[...].astype(F32)
    r = lax.broadcasted_iota(jnp.int32, (tm, tm + 2 * halo), 0) + i * tm
    s = lax.broadcasted_iota(jnp.int32, (tm, tm + 2 * halo), 1) + (i * tm - halo)
    seq_lo = jnp.where(r < L, 0, L)
    seq_hi = jnp.where(r < L, L, R)
    r1 = lax.broadcasted_iota(jnp.int32, (tm, LANES), 0) + i * tm
    seq_lo1 = jnp.where(r1 < L, 0, L)
    seq_hi1 = jnp.where(r1 < L, L, R)
    for g, w in enumerate(POOL_WINDOWS):
        sl = slice(g * LANES, (g + 1) * LANES)
        lo = jnp.maximum(r - w // 2, seq_lo)
        hi = jnp.minimum(r + w - w // 2, seq_hi)
        band = ((s >= lo) & (s < hi)).astype(BF16)
        cnt = (jnp.minimum(r1 + w - w // 2, seq_hi1) - jnp.maximum(r1 - w // 2, seq_lo1)).astype(F32)
        mean = _dot(band, u3[:, sl]) / cnt
        d = (mean - us[:, sl]).astype(BF16)
        o_ref[:, sl] = (_dot(d, w_ref[g]) * sc_ref[:, sl]).astype(BF16)


def _pool(pc, w_pool, pool_scale, li, L):
    R = pc.shape[0]
    tm, halo = POOL_TM, POOL_HALO
    assert max(POOL_WINDOWS) // 2 <= halo and tm % halo == 0
    nb = R // tm
    hb = tm // halo
    cb = PC_DX // 512
    return pl.pallas_call(
        functools.partial(_pool_kernel, L=L, R=R),
        grid=(nb,),
        in_specs=[
            pl.BlockSpec((halo, 512), lambda i: (jnp.maximum(i * hb - 1, 0), cb)),
            pl.BlockSpec((tm, 512), lambda i: (i, cb)),
            pl.BlockSpec((halo, 512), lambda i: (jnp.minimum((i + 1) * hb, R // halo - 1), cb)),
            pl.BlockSpec((None, len(POOL_WINDOWS), LANES, LANES), lambda i: (li, 0, 0, 0)),
            pl.BlockSpec((None, 1, 512), lambda i: (li, 0, 0)),
        ],
        out_specs=pl.BlockSpec((tm, 512), lambda i: (i, 0)),
        out_shape=jax.ShapeDtypeStruct((R, 512), BF16),
        compiler_params=_cparams(("arbitrary",)),
        name="pool",
    )(pc, pc, pc, w_pool, pool_scale)


def _merge_kernel(oa_ref, ob_ref, ohf_ref, ohb_ref, od_ref, gt_ref, mg0_ref, mg1_ref, mg2_ref, mg3_ref,
                  wbr_ref, wout_ref, hn_ref, x_ref, mod_ref, gpost_ref, gnext_ref, modn_ref,
                  xo_ref, ho_ref, *, ctx_tiles):
    is_ctx = pl.program_id(0) < ctx_tiles
    hs = ohf_ref[...] + ohb_ref[...]
    hn = hn_ref[...]
    oh = jnp.concatenate(
        [_rms(hs[:, h * C_DK:(h + 1) * C_DK], hn[:, h * C_DK:(h + 1) * C_DK]) for h in range(C_HEADS)], axis=1)
    outs = (oa_ref[...].astype(F32), ob_ref[...].astype(F32), oh, od_ref[...].astype(F32))
    mgs = (mg0_ref, mg1_ref, mg2_ref, mg3_ref)
    merged = None
    for nbr in range(N_BRANCH):
        gate = gt_ref[:, nbr * BRANCH_W:(nbr + 1) * BRANCH_W].astype(F32)
        ys = (outs[nbr] * (gate * _sigmoid(gate))).astype(BF16)
        yb = _dot(ys, wbr_ref[nbr])
        term = _sigmoid(mgs[nbr][...].astype(F32)) * yb
        merged = term if merged is None else merged + term
    y = _dot(merged.astype(BF16), wout_ref[...])
    _, _, gate_r = _mod_rows(mod_ref, is_ctx)
    xn = x_ref[...] + gate_r * _rms(y, gpost_ref[...])
    xo_ref[...] = xn
    shift_n, scale_n, _ = _mod_rows(modn_ref, is_ctx)
    ho_ref[...] = (_rms(xn, gnext_ref[...]) * (1.0 + scale_n) + shift_n).astype(BF16)


def _merge(oa, ob, ohf, ohb, od, pb, wbr, wout, hnorm, xs, mod, g_post, g_pre, li, L):
    R = xs.shape[0]
    tm = 256
    ln = min(li + 1, DEPTH - 1)
    row = lambda i: (i, 0)
    gcol = 0
    mcol = 1
    return pl.pallas_call(
        functools.partial(_merge_kernel, ctx_tiles=L // tm),
        grid=(R // tm,),
        in_specs=[
            pl.BlockSpec((tm, 512), row),
            pl.BlockSpec((tm, 512), row),
            pl.BlockSpec((tm, 512), row),
            pl.BlockSpec((tm, 512), row),
            pl.BlockSpec((tm, 512), row),
            pl.BlockSpec((tm, D_MODEL), lambda i: (i, gcol)),
            pl.BlockSpec((tm, D_MODEL), lambda i: (i, mcol)),
            pl.BlockSpec((tm, D_MODEL), lambda i: (i, mcol + 1)),
            pl.BlockSpec((tm, D_MODEL), lambda i: (i, mcol + 2)),
            pl.BlockSpec((tm, D_MODEL), lambda i: (i, mcol + 3)),
            pl.BlockSpec((None, N_BRANCH, BRANCH_W, D_MODEL), lambda i: (li, 0, 0, 0)),
            pl.BlockSpec((None, D_MODEL, D_MODEL), lambda i: (li, 0, 0)),
            pl.BlockSpec((None, 1, 512), lambda i: (li, 0, 0)),
            pl.BlockSpec((tm, D_MODEL), row),
            pl.BlockSpec((None, 8, 3 * D_MODEL), lambda i: (li, 0, 0)),
            pl.BlockSpec((None, 1, D_MODEL), lambda i: (li, 0, 0)),
            pl.BlockSpec((None, 1, D_MODEL), lambda i: (ln, 0, 0)),
            pl.BlockSpec((None, 8, 3 * D_MODEL), lambda i: (ln, 0, 0)),
        ],
        out_specs=[pl.BlockSpec((tm, D_MODEL), row), pl.BlockSpec((tm, D_MODEL), row)],
        out_shape=[jax.ShapeDtypeStruct((R, D_MODEL), F32), jax.ShapeDtypeStruct((R, D_MODEL), BF16)],
        compiler_params=_cparams(("arbitrary",)),
        name="merge",
    )(oa, ob, ohf, ohb, od, pb, pb, pb, pb, pb, wbr, wout, hnorm, xs, mod, g_post, g_pre, mod)


def _rope_tables(T, L):
    rows = jnp.repeat(jnp.arange(T // GRID_W, dtype=jnp.int32), GRID_W)
    cols = jnp.tile(jnp.arange(GRID_W, dtype=jnp.int32), T // GRID_W)
    half = 16
    freqs = ROPE_BASE ** (-jnp.arange(half, dtype=F32) / half)
    ar = rows.astype(F32)[:, None] * freqs[None, :]
    ac = cols.astype(F32)[:, None] * freqs[None, :]
    cos = jnp.concatenate([jnp.cos(ar), jnp.cos(ar), jnp.cos(ac), jnp.cos(ac)], axis=1)
    sin = jnp.concatenate([-jnp.sin(ar), jnp.sin(ar), -jnp.sin(ac), jnp.sin(ac)], axis=1)
    cos = jnp.concatenate([jnp.ones((L, 64), F32), cos], axis=0)
    sin = jnp.concatenate([jnp.zeros((L, 64), F32), sin], axis=0)
    return jnp.tile(cos, (1, 2)), jnp.tile(sin, (1, 2))


def kernel(x, c, ctx, c_ctx, w_ada, b_ada, g_pre, g_post, w_in, a_sink, mla_q_norm, w_uq, mla_kv_norm,
           w_ukv, hgrn_lb, hgrn_norm, w_pool, pool_scale, w_branch, w_out):
    assert x.shape[0] == 1 and ctx.shape[0] == 1
    T = x.shape[1]
    L = ctx.shape[1]
    assert T % 256 == 0 and L % 256 == 0 and T % GRID_W == 0

    w_t = jnp.swapaxes(w_in.astype(F32), 1, 2)
    wuq = w_uq.astype(BF16).reshape(DEPTH, B_Q_LORA, B_HEADS, B_NOPE + B_ROPE)
    wuq = jnp.pad(wuq, ((0, 0), (0, 512 - B_Q_LORA), (0, 0), (0, 256 - B_NOPE - B_ROPE))).reshape(DEPTH, 512, 1024)
    qn = jnp.pad(mla_q_norm.astype(F32), ((0, 0), (0, 512 - B_Q_LORA))).reshape(DEPTH, 1, 512)
    kn = mla_kv_norm.astype(F32).reshape(DEPTH, 1, B_KV_LORA)
    wukv = w_ukv.astype(BF16)
    wpool = w_pool.astype(BF16)
    pscale = pool_scale.astype(F32).reshape(DEPTH, 1, BRANCH_W)
    wbr = w_branch.astype(BF16)
    wout = w_out.astype(BF16)
    hnorm = hgrn_norm.astype(F32).reshape(DEPTH, 1, C_HEADS * C_DK)
    gpre = g_pre.astype(F32).reshape(DEPTH, 1, D_MODEL)
    gpost = g_post.astype(F32).reshape(DEPTH, 1, D_MODEL)

    cos, sin = _rope_tables(T, L)

    cc = jnp.concatenate([c.astype(F32), c_ctx.astype(F32)[None, :], jnp.zeros((6, D_MODEL), F32)], axis=0)
    mod = _ada_all(cc, w_ada, b_ada)
    lb = _lower_bounds(hgrn_lb)

    xs = jnp.concatenate([ctx[0], x[0]], axis=0).astype(F32)
    h = _prenorm(xs, gpre, mod, 0, L)
    for li in range(DEPTH):
        p1 = _proj(h, w_t, li, 0, O_P1_END, 512)
        pa = _proj(h, w_t, li, O_PA, PA_WIDTH, 1280)
        pc = _proj(h, w_t, li, O_PC, O_PB - O_PC, 1024)
        pb = _proj(h, w_t, li, O_PB, O_END - O_PB, 1024)
        aq, ak, mq, mk, mv = _prep(p1, pa, cos, sin, qn, wuq, kn, wukv, li)
        oa = _window_attn(a_sink[li].astype(F32), aq, ak, p1, L)
        ob = _mla_attn(mq, mk, mv, L)
        ohf, ohb = _hgrn(pa, pc, lb, li, L)
        od = _pool(pc, wpool, pscale, li, L)
        xs, h = _merge(oa, ob, ohf, ohb, od, pb, wbr, wout, hnorm, xs, mod, gpost, gpre, li, L)
    return xs[L:][None].astype(x.dtype)
```

```python
import functools

import jax
import jax.numpy as jnp
from jax import lax
from jax.experimental import pallas as pl
from jax.experimental.pallas import tpu as pltpu

F32 = jnp.float32
BF16 = jnp.bfloat16

D_MODEL = 2048
DEPTH = 4
GRID_W = 64
EPS = 1e-6
ROPE_BASE = 10000.0
N_BRANCH = 4
BRANCH_W = 512
A_HEADS = 8
A_HEAD_DIM = 64
WINDOW = 128
B_HEADS = 4
B_NOPE = 128
B_ROPE = 64
B_VDIM = 128
B_Q_LORA = 384
B_KV_LORA = 128
C_HEADS = 4
C_DK = 128
POOL_WINDOWS = (2, 4, 8, 16)

O_P1_END = 512
O_PA = 448
PA_WIDTH = 2560
O_PC = 2880
O_PB = 3904
O_END = 14144
PA_CFF = 0
PA_CI = 1024
PA_AQ = 1536
PA_CQ = 2048
PC_HQ = 0
PC_DX = 512

BLK = 128
LANES = 128
VMEM_LIMIT = 56 * 1024 * 1024

NT_DIMS = (((1,), (1,)), ((), ()))
LOG2E = 1.4426950408889634


def _cparams(sem):
    return pltpu.CompilerParams(dimension_semantics=sem, vmem_limit_bytes=VMEM_LIMIT)


def _pick_tile(n, cap, mult=BLK):
    best = mult
    t = mult
    while t <= min(n, cap):
        if n % t == 0:
            best = t
        t += mult
    return best


def _dot(a, b):
    return jnp.dot(a, b, preferred_element_type=F32)


def _dot_nt(a, b):
    return lax.dot_general(a, b, NT_DIMS, preferred_element_type=F32)


def _sigmoid(x):
    return 0.5 * jnp.tanh(0.5 * x) + 0.5


def _ada_kernel(cc_ref, w_ref, b_ref, o_ref):
    cc = cc_ref[...]
    s = (cc * _sigmoid(cc)).astype(BF16)
    o_ref[...] = _dot(s, w_ref[...].astype(BF16)) + b_ref[...]


def _ada_all(cc, w_ada, b_ada):
    tn = 1024
    n3 = 3 * D_MODEL
    return pl.pallas_call(
        _ada_kernel,
        grid=(DEPTH, n3 // tn),
        in_specs=[
            pl.BlockSpec((8, D_MODEL), lambda l, j: (0, 0)),
            pl.BlockSpec((None, D_MODEL, tn), lambda l, j: (l, 0, j)),
            pl.BlockSpec((None, 1, tn), lambda l, j: (l, 0, j)),
        ],
        out_specs=pl.BlockSpec((None, 8, tn), lambda l, j: (l, 0, j)),
        out_shape=jax.ShapeDtypeStruct((DEPTH, 8, n3), F32),
        compiler_params=_cparams(("arbitrary", "arbitrary")),
        name="adaln",
    )(cc, w_ada, b_ada.reshape(DEPTH, 1, n3))


def _lb_kernel(x_ref, o_ref):
    x = x_ref[...]
    m = jnp.max(x, axis=0, keepdims=True)
    e = jnp.exp(x - m)
    sm = e / jnp.sum(e, axis=0, keepdims=True)
    rows = []
    run = sm[0:1]
    first = run
    for l in range(DEPTH):
        if l > 0:
            run = run + sm[l:l + 1]
        rows.append(run - first)
    o_ref[...] = jnp.concatenate(rows, axis=0)


def _lower_bounds(hgrn_lb):
    n = 2 * C_HEADS * C_DK
    out = pl.pallas_call(
        _lb_kernel,
        out_shape=jax.ShapeDtypeStruct((DEPTH, n), F32),
        name="hgrn_lb",
    )(hgrn_lb.reshape(DEPTH, n).astype(F32))
    return out.reshape(DEPTH, 2, 1, C_HEADS * C_DK)


def _rms(x, g):
    return x * lax.rsqrt(jnp.mean(x * x, axis=-1, keepdims=True) + EPS) * g


def _mod_rows(mod_ref, is_ctx):
    m = mod_ref[...]
    sel = jnp.where(is_ctx, m[1:2, :], m[0:1, :])
    return sel[:, 0:D_MODEL], sel[:, D_MODEL:2 * D_MODEL], sel[:, 2 * D_MODEL:3 * D_MODEL]


def _prenorm_kernel(xc_ref, xl_ref, g_ref, mod_ref, h_ref, *, ctx_tiles):
    is_ctx = pl.program_id(0) < ctx_tiles
    shift, scale, _ = _mod_rows(mod_ref, is_ctx)
    x = jnp.where(is_ctx, xc_ref[...], xl_ref[...])
    h = _rms(x, g_ref[...]) * (1.0 + scale) + shift
    h_ref[...] = h.astype(BF16)


def _ctx_row(ct):
    return lambda i: (jnp.minimum(i, ct - 1), 0)


def _lat_row(ct):
    return lambda i: (jnp.maximum(i - ct, 0), 0)


def _prenorm(xc, xl, g_pre, mod, li):
    L, T = xc.shape[0], xl.shape[0]
    tm = 256
    ct = L // tm
    return pl.pallas_call(
        functools.partial(_prenorm_kernel, ctx_tiles=ct),
        grid=((L + T) // tm,),
        in_specs=[
            pl.BlockSpec((tm, D_MODEL), _ctx_row(ct)),
            pl.BlockSpec((tm, D_MODEL), _lat_row(ct)),
            pl.BlockSpec((None, 1, D_MODEL), lambda i: (li, 0, 0)),
            pl.BlockSpec((None, 8, 3 * D_MODEL), lambda i: (li, 0, 0)),
        ],
        out_specs=pl.BlockSpec((tm, D_MODEL), lambda i: (i, 0)),
        out_shape=jax.ShapeDtypeStruct((L + T, D_MODEL), BF16),
        compiler_params=_cparams(("arbitrary",)),
        name="prenorm",
    )(xc, xl, g_pre, mod)


def _proj_kernel(h_ref, wt_ref, o_ref, wbf_ref):
    @pl.when(pl.program_id(1) == 0)
    def _():
        wbf_ref[...] = wt_ref[...].astype(BF16)

    o_ref[...] = _dot_nt(h_ref[...], wbf_ref[...]).astype(o_ref.dtype)


def _proj(h, w_t, li, start, width, tn):
    R = h.shape[0]
    tm = _pick_tile(R, 1408)
    assert width % tn == 0 and start % 8 == 0 and start + width <= w_t.shape[1]
    return pl.pallas_call(
        _proj_kernel,
        grid=(width // tn, R // tm),
        in_specs=[
            pl.BlockSpec((tm, D_MODEL), lambda j, i: (i, 0)),
            pl.BlockSpec((None, pl.Element(tn), pl.Element(D_MODEL)),
                         lambda j, i: (li, pl.multiple_of(start + j * tn, 8), 0)),
        ],
        out_specs=pl.BlockSpec((tm, tn), lambda j, i: (i, j)),
        out_shape=jax.ShapeDtypeStruct((R, width), BF16),
        scratch_shapes=[pltpu.VMEM((tn, D_MODEL), BF16)],
        compiler_params=_cparams(("arbitrary", "arbitrary")),
        name="proj",
    )(h, w_t)


def _rope_tile(t, cos, sin):
    lane = lax.broadcasted_iota(jnp.int32, t.shape, 1)
    first = (lane & 16) == 0
    partner = jnp.where(first, pltpu.roll(t, LANES - 16, 1), pltpu.roll(t, 16, 1))
    return t * cos + partner * sin


def _prep_kernel(aq_ref, kv_ref, cq_ref, cos_ref, sin_ref, qn_ref, wuq_ref, kn_ref, wukv_ref,
                 aq_o, ak_o, qc_o, ql_o, k_o, v_o, *, ctx_tiles):
    cos = cos_ref[...]
    sin = sin_ref[...]
    a_scale = A_HEAD_DIM ** -0.5
    for t in range(A_HEADS * A_HEAD_DIM // LANES):
        sl = slice(t * LANES, (t + 1) * LANES)
        aq_o[:, sl] = (_rope_tile(aq_ref[:, sl].astype(F32), cos, sin) * a_scale).astype(BF16)
    ak_o[...] = _rope_tile(kv_ref[:, 0:128].astype(F32), cos, sin).astype(BF16)

    ckv = kv_ref[:, 256:384].astype(F32)
    ckv = _rms(ckv, kn_ref[...])
    kv = _dot(ckv.astype(BF16), wukv_ref[...])
    krt = kv_ref[:, 384:512].astype(F32)
    krt = jnp.where(lax.broadcasted_iota(jnp.int32, krt.shape, 1) < B_ROPE, krt, 0.0)
    kr = _rope_tile(krt, cos, sin).astype(BF16)
    for h in range(B_HEADS):
        k_o[h, :, 0:128] = kv[:, 256 * h:256 * h + 128].astype(BF16)
        k_o[h, :, 128:256] = kr
        v_o[h, :, 0:128] = kv[:, 256 * h + 128:256 * h + 256].astype(BF16)
        v_o[h, :, 128:256] = jnp.ones((kv.shape[0], 128), BF16)

    cq = cq_ref[...].astype(F32)
    cq = jnp.where(lax.broadcasted_iota(jnp.int32, cq.shape, 1) < B_Q_LORA, cq, 0.0)
    cq = cq * lax.rsqrt(jnp.sum(cq * cq, axis=-1, keepdims=True) * (1.0 / B_Q_LORA) + EPS) * qn_ref[...]
    q = _dot(cq.astype(BF16), wuq_ref[...])
    b_scale = (B_NOPE + B_ROPE) ** -0.5 * LOG2E
    qn = [(q[:, 256 * h:256 * h + 128] * b_scale).astype(BF16) for h in range(B_HEADS)]
    qr = [(_rope_tile(q[:, 256 * h + 128:256 * h + 256], cos, sin) * b_scale).astype(BF16) for h in range(B_HEADS)]

    def put(q_o):
        for h in range(B_HEADS):
            q_o[h, :, 0:128] = qn[h]
            q_o[h, :, 128:256] = qr[h]

    is_ctx = pl.program_id(0) < ctx_tiles
    pl.when(is_ctx)(lambda: put(qc_o))
    pl.when(jnp.logical_not(is_ctx))(lambda: put(ql_o))


def _prep(p1, pa, cos, sin, qn, wuq, kn, wukv, li, L):
    R = p1.shape[0]
    tm = 256
    ct = L // tm
    row = lambda i: (i, 0)
    return pl.pallas_call(
        functools.partial(_prep_kernel, ctx_tiles=ct),
        grid=(R // tm,),
        in_specs=[
            pl.BlockSpec((tm, 512), lambda i: (i, PA_AQ // 512)),
            pl.BlockSpec((tm, 512), row),
            pl.BlockSpec((tm, 512), lambda i: (i, PA_CQ // 512)),
            pl.BlockSpec((tm, LANES), row),
            pl.BlockSpec((tm, LANES), row),
            pl.BlockSpec((None, 1, 512), lambda i: (li, 0, 0)),
            pl.BlockSpec((None, 512, 1024), lambda i: (li, 0, 0)),
            pl.BlockSpec((None, 1, B_KV_LORA), lambda i: (li, 0, 0)),
            pl.BlockSpec((None, B_KV_LORA, 1024), lambda i: (li, 0, 0)),
        ],
        out_specs=[
            pl.BlockSpec((tm, 512), row),
            pl.BlockSpec((tm, LANES), row),
            pl.BlockSpec((B_HEADS, tm, 256), lambda i: (0, jnp.minimum(i, ct - 1), 0)),
            pl.BlockSpec((B_HEADS, tm, 256), lambda i: (0, jnp.maximum(i - ct, 0), 0)),
            pl.BlockSpec((B_HEADS, tm, 256), lambda i: (0, i, 0)),
            pl.BlockSpec((B_HEADS, tm, 256), lambda i: (0, i, 0)),
        ],
        out_shape=[
            jax.ShapeDtypeStruct((R, 512), BF16),
            jax.ShapeDtypeStruct((R, LANES), BF16),
            jax.ShapeDtypeStruct((B_HEADS, L, 256), BF16),
            jax.ShapeDtypeStruct((B_HEADS, R - L, 256), BF16),
            jax.ShapeDtypeStruct((B_HEADS, R, 256), BF16),
            jax.ShapeDtypeStruct((B_HEADS, R, 256), BF16),
        ],
        compiler_params=_cparams(("arbitrary",)),
        name="prep",
    )(pa, p1, pa, cos, sin, qn, wuq, kn, wukv)


def _swap_halves(t):
    return jnp.concatenate([t[:, 64:], t[:, :64]], axis=1)


WIN_TM = 2 * BLK


def _win_kernel(sink_ref, q_ref, kp_ref, ks_ref, kn_ref, vp_ref, vs_ref, vn_ref, kc_ref, vc_ref, o_ref,
                *, L, R):
    i = pl.program_id(0)
    k_all = jnp.concatenate([kp_ref[...], ks_ref[...], kn_ref[...]], axis=0)
    v_all = jnp.concatenate([vp_ref[...], vs_ref[...], vn_ref[...]], axis=0)
    kc = kc_ref[...]
    vc = vc_ref[...]
    lo = lax.broadcasted_iota(jnp.int32, (BLK, LANES), 1) < 64
    zero = jnp.zeros((BLK, LANES), BF16)
    rows = 4 * BLK
    rr = lax.broadcasted_iota(jnp.int32, (rows, 3 * BLK), 0)
    cc = lax.broadcasted_iota(jnp.int32, (rows, 3 * BLK), 1)
    hrow = lax.broadcasted_iota(jnp.int32, (rows, 1), 0) // BLK

    for sb in range(WIN_TM // BLK):
        bi = i * (WIN_TM // BLK) + sb
        q = q_ref[sb * BLK:(sb + 1) * BLK, :]
        kl = k_all[sb * BLK:(sb + 3) * BLK]
        vl = v_all[sb * BLK:(sb + 3) * BLK]
        qpos = (rr & (BLK - 1)) + bi * BLK
        kpos = cc + (bi - 1) * BLK
        ok = ((kpos >= L) & (kpos < R) & (qpos >= L)
              & (kpos - qpos <= WINDOW) & (qpos - kpos <= WINDOW))
        for g in range(2):
            t0 = q[:, 256 * g:256 * g + 128]
            t1 = q[:, 256 * g + 128:256 * g + 256]
            if g == 0:
                parts = [jnp.where(lo, t0, zero), jnp.where(lo, _swap_halves(t0), zero),
                         jnp.where(lo, t1, zero), jnp.where(lo, _swap_halves(t1), zero)]
            else:
                parts = [jnp.where(lo, zero, _swap_halves(t0)), jnp.where(lo, zero, t0),
                         jnp.where(lo, zero, _swap_halves(t1)), jnp.where(lo, zero, t1)]
            qs = jnp.concatenate(parts, axis=0)
            s_l = jnp.where(ok, _dot_nt(qs, kl), -jnp.inf)
            s_c = _dot_nt(qs, kc)
            sink = jnp.zeros((rows, 1), F32)
            for j in range(4):
                sink = jnp.where(hrow == j, sink_ref[4 * g + j], sink)
            m = jnp.maximum(jnp.maximum(jnp.max(s_l, axis=-1, keepdims=True),
                                        jnp.max(s_c, axis=-1, keepdims=True)), sink)
            p_l = jnp.exp(s_l - m)
            p_c = jnp.exp(s_c - m)
            den = (jnp.sum(p_l, axis=-1, keepdims=True) + jnp.sum(p_c, axis=-1, keepdims=True)
                   + jnp.exp(sink - m))
            o = (_dot(p_l.astype(BF16), vl) + _dot(p_c.astype(BF16), vc)) / den
            for pair in range(2):
                a = o[2 * pair * BLK:(2 * pair + 1) * BLK]
                b = o[(2 * pair + 1) * BLK:(2 * pair + 2) * BLK]
                if g == 0:
                    tile = jnp.where(lo, a, _swap_halves(b))
                else:
                    tile = jnp.where(lo, _swap_halves(a), b)
                c0 = 256 * g + 128 * pair
                o_ref[sb * BLK:(sb + 1) * BLK, c0:c0 + 128] = tile.astype(BF16)


def _window_attn(sink, aq, ak, p, L):
    R = aq.shape[0]
    nb = R // BLK
    per = WIN_TM // BLK
    prev = lambda c: (lambda i: (jnp.maximum(i * per - 1, 0), c))
    this = lambda c: (lambda i: (i, c))
    nxt = lambda c: (lambda i: (jnp.minimum((i + 1) * per, nb - 1), c))
    return pl.pallas_call(
        functools.partial(_win_kernel, L=L, R=R),
        grid=(R // WIN_TM,),
        in_specs=[
            pl.BlockSpec(memory_space=pltpu.SMEM),
            pl.BlockSpec((WIN_TM, 512), lambda i: (i, 0)),
            pl.BlockSpec((BLK, LANES), prev(0)),
            pl.BlockSpec((WIN_TM, LANES), this(0)),
            pl.BlockSpec((BLK, LANES), nxt(0)),
            pl.BlockSpec((BLK, LANES), prev(1)),
            pl.BlockSpec((WIN_TM, LANES), this(1)),
            pl.BlockSpec((BLK, LANES), nxt(1)),
            pl.BlockSpec((L, LANES), lambda i: (0, 0)),
            pl.BlockSpec((L, LANES), lambda i: (0, 1)),
        ],
        out_specs=pl.BlockSpec((WIN_TM, 512), lambda i: (i, 0)),
        out_shape=jax.ShapeDtypeStruct((R, 512), BF16),
        compiler_params=_cparams(("arbitrary",)),
        name="window_attn",
    )(sink, aq, ak, ak, ak, p, p, p, ak, p)


def _softmax_out(acc):
    return (acc[:, 0:B_VDIM] / acc[:, B_VDIM:2 * B_VDIM]).astype(BF16)


def _mla_kernel(ql_ref, qc_ref, k_ref, v_ref, ol_ref, oc_ref, s0_ref, s1_ref, m0_ref, m1_ref, *, L, tq, tk):
    i = pl.program_id(1)
    nk = k_ref.shape[0] // tk

    @pl.when(i == 0)
    def _():
        s1_ref[...] = jnp.zeros_like(s1_ref)
        m1_ref[...] = jnp.zeros_like(m1_ref)

    def step(sw_ref, mw_ref, sr_ref, mr_ref):
        q = ql_ref[...]
        m_prev = mr_ref[...]
        mrun = jnp.full((tq, LANES), -jnp.inf, F32)
        acc = jnp.zeros((tq, 2 * B_VDIM), F32)
        for c in range(nk):
            rows = slice(c * tk, (c + 1) * tk)
            s = _dot_nt(q, k_ref[rows, :])
            sw_ref[c] = s
            for t in range(tk // LANES):
                mrun = jnp.maximum(mrun, s[:, t * LANES:(t + 1) * LANES])
            sp = sr_ref[c]
            p = jnp.concatenate(
                [jnp.exp2(sp[:, t * LANES:(t + 1) * LANES] - m_prev) for t in range(tk // LANES)], axis=1)
            acc = acc + _dot(p.astype(BF16), v_ref[rows, :])
        mw_ref[...] = jnp.broadcast_to(jnp.max(mrun, axis=-1, keepdims=True), (tq, LANES))
        ol_ref[...] = _softmax_out(acc)

    @pl.when(i % 2 == 0)
    def _():
        step(s0_ref, m0_ref, s1_ref, m1_ref)

    @pl.when(i % 2 == 1)
    def _():
        step(s1_ref, m1_ref, s0_ref, m0_ref)

    @pl.when(i == 0)
    def _():
        s = _dot_nt(qc_ref[...], k_ref[0:L, :])
        p = jnp.exp2(s - jnp.max(s, axis=-1, keepdims=True))
        oc_ref[...] = _softmax_out(_dot(p.astype(BF16), v_ref[0:L, :]))


def _mla_attn(q_ctx, q_lat, k, v):
    L, T, R = q_ctx.shape[1], q_lat.shape[1], k.shape[1]
    tq = _pick_tile(T, 512, 256)
    tk = 256
    nt = T // tq
    once = pl.Buffered(1)
    return pl.pallas_call(
        functools.partial(_mla_kernel, L=L, tq=tq, tk=tk),
        grid=(B_HEADS, nt + 1),
        in_specs=[
            pl.BlockSpec((None, tq, 256), lambda h, i: (h, jnp.minimum(i, nt - 1), 0)),
            pl.BlockSpec((None, L, 256), lambda h, i: (h, 0, 0)),
            pl.BlockSpec((None, R, 256), lambda h, i: (h, 0, 0), pipeline_mode=once),
            pl.BlockSpec((None, R, 256), lambda h, i: (h, 0, 0), pipeline_mode=once),
        ],
        out_specs=[
            pl.BlockSpec((tq, B_VDIM), lambda h, i: (jnp.maximum(i - 1, 0), h)),
            pl.BlockSpec((L, B_VDIM), lambda h, i: (0, h)),
        ],
        out_shape=[
            jax.ShapeDtypeStruct((T, B_HEADS * B_VDIM), BF16),
            jax.ShapeDtypeStruct((L, B_HEADS * B_VDIM), BF16),
        ],
        scratch_shapes=[
            pltpu.VMEM((R // tk, tq, tk), F32),
            pltpu.VMEM((R // tk, tq, tk), F32),
            pltpu.VMEM((tq, LANES), F32),
            pltpu.VMEM((tq, LANES), F32),
        ],
        compiler_params=_cparams(("arbitrary", "arbitrary")),
        name="mla_attn",
    )(q_lat, q_ctx, k, v)


LEVELS = (64, 32, 16, 8, 4, 2, 1)


def _split3(x):
    hi = x.astype(BF16)
    r = x - hi.astype(F32)
    mid = r.astype(BF16)
    lo = (r - mid.astype(F32)).astype(BF16)
    return hi, mid, lo


def _hgrn_kernel(zf_ref, vf_ref, qf_ref, zb_ref, vb_ref, qb_ref, lb_ref, of_ref, ob_ref, st_ref):
    n = pl.program_id(0)

    @pl.when(n == 0)
    def _():
        st_ref[...] = jnp.zeros_like(st_ref)

    half = C_HEADS * C_DK
    width = 2 * half
    nh = 2 * C_HEADS
    heads = [slice(h * C_DK, (h + 1) * C_DK) for h in range(nh)]
    r0 = lax.broadcasted_iota(jnp.int32, (BLK, BLK), 0)
    c0 = lax.broadcasted_iota(jnp.int32, (BLK, BLK), 1)
    lt = (r0 >= c0).astype(BF16)
    rev = (r0 + c0 == BLK - 1).astype(BF16)
    ones = jnp.ones((BLK, BLK), BF16)
    sub8 = lax.broadcasted_iota(jnp.int32, (BLK // 8, 8, width), 1)

    z = jnp.concatenate([zf_ref[...].astype(F32), _dot(rev, zb_ref[...])], axis=1)
    v = jnp.concatenate([vf_ref[...].astype(F32), _dot(rev, vb_ref[...])], axis=1)
    vb = v.astype(BF16)
    qp = jnp.concatenate([qf_ref[...].astype(F32), _dot(rev, qb_ref[...])], axis=1)
    lb = jnp.concatenate([lb_ref[0], lb_ref[1]], axis=1)

    ls = jnp.minimum(z, 0.0) - jnp.log(1.0 + jnp.exp(-jnp.abs(z)))
    a1 = jnp.log(lb)
    a2 = jnp.log1p(-lb) + ls
    lf = jnp.maximum(a1, a2) + jnp.log(1.0 + jnp.exp(-jnp.abs(a1 - a2)))
    k = (1.0 - lb) * (1.0 / (1.0 + jnp.exp(z)))
    q = qp * _sigmoid(qp) * (C_DK ** -0.5)

    hi, mid, lo = _split3(lf * LOG2E)
    b = _dot(lt, hi) + _dot(lt, mid) + _dot(lt, lo)

    sts = [st_ref[h] for h in range(nh)]
    qe = (q * jnp.exp2(b)).astype(BF16)
    outs = [_dot_nt(qe[:, sl], st.astype(BF16)) for sl, st in zip(heads, sts)]

    b8 = b.reshape(BLK // 8, 8, width)
    atts = [jnp.zeros((BLK, BLK), F32) for _ in heads]
    for s in LEVELS:
        if s >= 8:
            g = BLK // (2 * s)
            e = jnp.broadcast_to(b.reshape(g, 2 * s, width)[:, s - 1:s, :], (g, 2 * s, width))
        elif s == 4:
            e = jnp.broadcast_to(b8[:, 3:4, :], b8.shape)
        elif s == 2:
            e = jnp.where(sub8 < 4, jnp.broadcast_to(b8[:, 1:2, :], b8.shape),
                          jnp.broadcast_to(b8[:, 5:6, :], b8.shape))
        else:
            e = jnp.where((sub8 & 1) == 0, b8, pltpu.roll(b8, 1, 1))
        w = jnp.exp2(-jnp.abs(b - e.reshape(BLK, width)))
        qw = q * w
        kw = k * w
        right = (r0 & s) != 0
        pair = (r0 // (2 * s)) == (c0 // (2 * s))
        atts = [jnp.where(pair,
                          _dot_nt(jnp.where(right, qw[:, sl], 0.0).astype(BF16),
                                  jnp.where(right, 0.0, kw[:, sl]).astype(BF16)), att)
                for sl, att in zip(heads, atts)]
    dg = (q * k).astype(BF16)
    outs = [o + _dot(att.astype(BF16), vb[:, sl]) + _dot(dg[:, sl], ones) * v[:, sl]
            for sl, o, att in zip(heads, outs, atts)]

    bl = b[BLK - 1:BLK, :]
    kl = (k * jnp.exp2(bl - b)).astype(BF16)
    dec = jnp.exp2(bl)
    for h, sl in enumerate(heads):
        st_ref[h] = sts[h] * dec[:, sl] + _dot(v[:, sl].T.astype(BF16), kl[:, sl])

    of_ref[...] = jnp.concatenate(outs[:C_HEADS], axis=1)
    o = jnp.concatenate(outs[C_HEADS:], axis=1)
    ohi = o.astype(BF16)
    olo = (o - ohi.astype(F32)).astype(BF16)
    ob_ref[...] = _dot(rev, ohi) + _dot(rev, olo)


def _hgrn(pa, pc, lb, li, L):
    R = pa.shape[0]
    nb = R // BLK
    nc = L // BLK

    def back(n):
        return jnp.where(n < nc, nc - 1 - n, nb - 1 - (n - nc))

    fwd = lambda c: (lambda n: (n, c))
    bwd = lambda c: (lambda n: (back(n), c))
    return pl.pallas_call(
        _hgrn_kernel,
        grid=(nb,),
        in_specs=[
            pl.BlockSpec((BLK, 512), fwd(PA_CFF // 512)),
            pl.BlockSpec((BLK, 512), fwd(PA_CI // 512)),
            pl.BlockSpec((BLK, 512), fwd(PC_HQ // 512)),
            pl.BlockSpec((BLK, 512), bwd(PA_CFF // 512 + 1)),
            pl.BlockSpec((BLK, 512), bwd(PA_CI // 512)),
            pl.BlockSpec((BLK, 512), bwd(PC_HQ // 512)),
            pl.BlockSpec((None, 2, 1, 512), lambda n: (li, 0, 0, 0)),
        ],
        out_specs=[pl.BlockSpec((BLK, 512), fwd(0)), pl.BlockSpec((BLK, 512), bwd(0))],
        out_shape=[jax.ShapeDtypeStruct((R, 512), F32), jax.ShapeDtypeStruct((R, 512), F32)],
        scratch_shapes=[pltpu.VMEM((2 * C_HEADS, C_DK, C_DK), F32)],
        compiler_params=_cparams(("arbitrary",)),
        name="hgrn",
    )(pa, pa, pc, pa, pa, pc, lb)


POOL_TM = 256
POOL_HALO = 16


def _pool_kernel(up_ref, us_ref, un_ref, w_ref, sc_ref, o_ref, *, L, R):
    i = pl.program_id(0)
    tm, halo = POOL_TM, POOL_HALO
    u3 = jnp.concatenate([up_ref[...], us_ref[...], un_ref[...]], axis=0)
    us = us_ref[...].astype(F32)
    r = lax.broadcasted_iota(jnp.int32, (tm, tm + 2 * halo), 0) + i * tm
    s = lax.broadcasted_iota(jnp.int32, (tm, tm + 2 * halo), 1) + (i * tm - halo)
    seq_lo = jnp.where(r < L, 0, L)
    seq_hi = jnp.where(r < L, L, R)
    r1 = lax.broadcasted_iota(jnp.int32, (tm, LANES), 0) + i * tm
    seq_lo1 = jnp.where(r1 < L, 0, L)
    seq_hi1 = jnp.where(r1 < L, L, R)
    for g, w in enumerate(POOL_WINDOWS):
        sl = slice(g * LANES, (g + 1) * LANES)
        lo = jnp.maximum(r - w // 2, seq_lo)
        hi = jnp.minimum(r + w - w // 2, seq_hi)
        band = ((s >= lo) & (s < hi)).astype(BF16)
        cnt = (jnp.minimum(r1 + w - w // 2, seq_hi1) - jnp.maximum(r1 - w // 2, seq_lo1)).astype(F32)
        mean = _dot(band, u3[:, sl]) / cnt
        d = (mean - us[:, sl]).astype(BF16)
        o_ref[:, sl] = (_dot(d, w_ref[g]) * sc_ref[:, sl]).astype(BF16)


def _pool(pc, w_pool, pool_scale, li, L):
    R = pc.shape[0]
    tm, halo = POOL_TM, POOL_HALO
    assert max(POOL_WINDOWS) // 2 <= halo and tm % halo == 0
    nb = R // tm
    hb = tm // halo
    cb = PC_DX // 512
    return pl.pallas_call(
        functools.partial(_pool_kernel, L=L, R=R),
        grid=(nb,),
        in_specs=[
            pl.BlockSpec((halo, 512), lambda i: (jnp.maximum(i * hb - 1, 0), cb)),
            pl.BlockSpec((tm, 512), lambda i: (i, cb)),
            pl.BlockSpec((halo, 512), lambda i: (jnp.minimum((i + 1) * hb, R // halo - 1), cb)),
            pl.BlockSpec((None, len(POOL_WINDOWS), LANES, LANES), lambda i: (li, 0, 0, 0)),
            pl.BlockSpec((None, 1, 512), lambda i: (li, 0, 0)),
        ],
        out_specs=pl.BlockSpec((tm, 512), lambda i: (i, 0)),
        out_shape=jax.ShapeDtypeStruct((R, 512), BF16),
        compiler_params=_cparams(("arbitrary",)),
        name="pool",
    )(pc, pc, pc, w_pool, pool_scale)


def _merge_kernel(oa_ref, obc_ref, obl_ref, ohf_ref, ohb_ref, od_ref, gt_ref, mg0_ref, mg1_ref, mg2_ref, mg3_ref,
                  wbr_ref, wout_ref, hn_ref, xc_ref, xl_ref, mod_ref, gpost_ref, gnext_ref, modn_ref,
                  xo_ref, *ho_refs, ctx_tiles):
    is_ctx = pl.program_id(0) < ctx_tiles
    hs = ohf_ref[...] + ohb_ref[...]
    hn = hn_ref[...]
    oh = jnp.concatenate(
        [_rms(hs[:, h * C_DK:(h + 1) * C_DK], hn[:, h * C_DK:(h + 1) * C_DK]) for h in range(C_HEADS)], axis=1)
    ob = jnp.where(is_ctx, obc_ref[...], obl_ref[...])
    outs = (oa_ref[...].astype(F32), ob.astype(F32), oh, od_ref[...].astype(F32))
    mgs = (mg0_ref, mg1_ref, mg2_ref, mg3_ref)
    merged = None
    for nbr in range(N_BRANCH):
        gate = gt_ref[:, nbr * BRANCH_W:(nbr + 1) * BRANCH_W].astype(F32)
        ys = (outs[nbr] * (gate * _sigmoid(gate))).astype(BF16)
        yb = _dot(ys, wbr_ref[nbr])
        term = _sigmoid(mgs[nbr][...].astype(F32)) * yb
        merged = term if merged is None else merged + term
    y = _dot(merged.astype(BF16), wout_ref[...])
    _, _, gate_r = _mod_rows(mod_ref, is_ctx)
    xn = jnp.where(is_ctx, xc_ref[...], xl_ref[...]) + gate_r * _rms(y, gpost_ref[...])
    xo_ref[...] = xn
    for ho_ref in ho_refs:
        shift_n, scale_n, _ = _mod_rows(modn_ref, is_ctx)
        ho_ref[...] = (_rms(xn, gnext_ref[...]) * (1.0 + scale_n) + shift_n).astype(BF16)


def row_all(i):
    return (i, 0)


def _merge(oa, obc, obl, ohf, ohb, od, pb, wbr, wout, hnorm, xc, xl, mod, g_post, g_pre, li, L):
    R = oa.shape[0]
    tm = 256
    ct = L // tm
    xl_row = _lat_row(ct) if xl.shape[0] == R - L else row_all
    last = li == DEPTH - 1
    ln = min(li + 1, DEPTH - 1)
    row = row_all
    gcol = 0
    mcol = 1
    if last:
        ct = L // tm
        out_specs = [pl.BlockSpec((tm, D_MODEL), lambda i: (jnp.maximum(i - ct, 0), 0))]
        out_shape = [jax.ShapeDtypeStruct((R - L, D_MODEL), F32)]
    else:
        out_specs = [pl.BlockSpec((tm, D_MODEL), row), pl.BlockSpec((tm, D_MODEL), row)]
        out_shape = [jax.ShapeDtypeStruct((R, D_MODEL), F32), jax.ShapeDtypeStruct((R, D_MODEL), BF16)]
    return pl.pallas_call(
        functools.partial(_merge_kernel, ctx_tiles=L // tm),
        grid=(R // tm,),
        in_specs=[
            pl.BlockSpec((tm, 512), row),
            pl.BlockSpec((tm, 512), _ctx_row(ct)),
            pl.BlockSpec((tm, 512), _lat_row(ct)),
            pl.BlockSpec((tm, 512), row),
            pl.BlockSpec((tm, 512), row),
            pl.BlockSpec((tm, 512), row),
            pl.BlockSpec((tm, D_MODEL), lambda i: (i, gcol)),
            pl.BlockSpec((tm, D_MODEL), lambda i: (i, mcol)),
            pl.BlockSpec((tm, D_MODEL), lambda i: (i, mcol + 1)),
            pl.BlockSpec((tm, D_MODEL), lambda i: (i, mcol + 2)),
            pl.BlockSpec((tm, D_MODEL), lambda i: (i, mcol + 3)),
            pl.BlockSpec((None, N_BRANCH, BRANCH_W, D_MODEL), lambda i: (li, 0, 0, 0)),
            pl.BlockSpec((None, D_MODEL, D_MODEL), lambda i: (li, 0, 0)),
            pl.BlockSpec((None, 1, 512), lambda i: (li, 0, 0)),
            pl.BlockSpec((tm, D_MODEL), _ctx_row(ct)),
            pl.BlockSpec((tm, D_MODEL), xl_row),
            pl.BlockSpec((None, 8, 3 * D_MODEL), lambda i: (li, 0, 0)),
            pl.BlockSpec((None, 1, D_MODEL), lambda i: (li, 0, 0)),
            pl.BlockSpec((None, 1, D_MODEL), lambda i: (ln, 0, 0)),
            pl.BlockSpec((None, 8, 3 * D_MODEL), lambda i: (ln, 0, 0)),
        ],
        out_specs=out_specs,
        out_shape=out_shape,
        compiler_params=_cparams(("arbitrary",)),
        name="merge",
    )(oa, obc, obl, ohf, ohb, od, pb, pb, pb, pb, pb, wbr, wout, hnorm, xc, xl, mod, g_post, g_pre, mod)


def _rope_tables(T, L):
    rows = jnp.repeat(jnp.arange(T // GRID_W, dtype=jnp.int32), GRID_W)
    cols = jnp.tile(jnp.arange(GRID_W, dtype=jnp.int32), T // GRID_W)
    half = 16
    freqs = ROPE_BASE ** (-jnp.arange(half, dtype=F32) / half)
    ar = rows.astype(F32)[:, None] * freqs[None, :]
    ac = cols.astype(F32)[:, None] * freqs[None, :]
    cos = jnp.concatenate([jnp.cos(ar), jnp.cos(ar), jnp.cos(ac), jnp.cos(ac)], axis=1)
    sin = jnp.concatenate([-jnp.sin(ar), jnp.sin(ar), -jnp.sin(ac), jnp.sin(ac)], axis=1)
    cos = jnp.concatenate([jnp.ones((L, 64), F32), cos], axis=0)
    sin = jnp.concatenate([jnp.zeros((L, 64), F32), sin], axis=0)
    return jnp.tile(cos, (1, 2)), jnp.tile(sin, (1, 2))


def kernel(x, c, ctx, c_ctx, w_ada, b_ada, g_pre, g_post, w_in, a_sink, mla_q_norm, w_uq, mla_kv_norm,
           w_ukv, hgrn_lb, hgrn_norm, w_pool, pool_scale, w_branch, w_out):
    assert x.shape[0] == 1 and ctx.shape[0] == 1
    T = x.shape[1]
    L = ctx.shape[1]
    assert T % 256 == 0 and L % 256 == 0 and T % GRID_W == 0

    w_t = jnp.swapaxes(w_in.astype(F32), 1, 2)
    wuq = w_uq.astype(BF16).reshape(DEPTH, B_Q_LORA, B_HEADS, B_NOPE + B_ROPE)
    wuq = jnp.pad(wuq, ((0, 0), (0, 512 - B_Q_LORA), (0, 0), (0, 256 - B_NOPE - B_ROPE))).reshape(DEPTH, 512, 1024)
    qn = jnp.pad(mla_q_norm.astype(F32), ((0, 0), (0, 512 - B_Q_LORA))).reshape(DEPTH, 1, 512)
    kn = mla_kv_norm.astype(F32).reshape(DEPTH, 1, B_KV_LORA)
    wukv = w_ukv.astype(BF16)
    wpool = w_pool.astype(BF16)
    pscale = pool_scale.astype(F32).reshape(DEPTH, 1, BRANCH_W)
    wbr = w_branch.astype(BF16)
    wout = w_out.astype(BF16)
    hnorm = hgrn_norm.astype(F32).reshape(DEPTH, 1, C_HEADS * C_DK)
    gpre = g_pre.astype(F32).reshape(DEPTH, 1, D_MODEL)
    gpost = g_post.astype(F32).reshape(DEPTH, 1, D_MODEL)

    cos, sin = _rope_tables(T, L)

    cc = jnp.concatenate([c.astype(F32), c_ctx.astype(F32)[None, :], jnp.zeros((6, D_MODEL), F32)], axis=0)
    mod = _ada_all(cc, w_ada, b_ada)
    lb = _lower_bounds(hgrn_lb)

    xc, xl = ctx[0].astype(F32), x[0].astype(F32)
    h = _prenorm(xc, xl, gpre, mod, 0)
    for li in range(DEPTH):
        p1 = _proj(h, w_t, li, 0, O_P1_END, 512)
        pa = _proj(h, w_t, li, O_PA, PA_WIDTH, 1280)
        pc = _proj(h, w_t, li, O_PC, O_PB - O_PC, 1024)
        pb = _proj(h, w_t, li, O_PB, O_END - O_PB, 1024)
        aq, ak, mqc, mql, mk, mv = _prep(p1, pa, cos, sin, qn, wuq, kn, wukv, li, L)
        oa = _window_attn(a_sink[li].astype(F32), aq, ak, p1, L)
        obl, obc = _mla_attn(mqc, mql, mk, mv)
        ohf, ohb = _hgrn(pa, pc, lb, li, L)
        od = _pool(pc, wpool, pscale, li, L)
        outs = _merge(oa, obc, obl, ohf, ohb, od, pb, wbr, wout, hnorm, xc, xl, mod, gpost, gpre, li, L)
        xc = xl = outs[0]
        h = outs[1] if len(outs) == 2 else None
    return xl[None].astype(x.dtype)
```

```python
import functools

import jax
import jax.numpy as jnp
from jax import lax
from jax.experimental import pallas as pl
from jax.experimental.pallas import tpu as pltpu

F32 = jnp.float32
BF16 = jnp.bfloat16

D_MODEL = 2048
DEPTH = 4
GRID_W = 64
EPS = 1e-6
ROPE_BASE = 10000.0
N_BRANCH = 4
BRANCH_W = 512
A_HEADS = 8
A_HEAD_DIM = 64
WINDOW = 128
B_HEADS = 4
B_NOPE = 128
B_ROPE = 64
B_VDIM = 128
B_Q_LORA = 384
B_KV_LORA = 128
C_HEADS = 4
C_DK = 128
POOL_WINDOWS = (2, 4, 8, 16)

O_P1_END = 512
O_PA = 448
PA_WIDTH = 2560
O_PC = 2880
O_PB = 3904
O_END = 14144
PA_CFF = 0
PA_CI = 1024
PA_AQ = 1536
PA_CQ = 2048
PC_HQ = 0
PC_DX = 512
HALF_GATES = 0.5

BLK = 128
LANES = 128
VMEM_LIMIT = 56 * 1024 * 1024

NT_DIMS = (((1,), (1,)), ((), ()))
LOG2E = 1.4426950408889634


def _cparams(sem):
    return pltpu.CompilerParams(dimension_semantics=sem, vmem_limit_bytes=VMEM_LIMIT)


def _pick_tile(n, cap, mult=BLK):
    best = mult
    t = mult
    while t <= min(n, cap):
        if n % t == 0:
            best = t
        t += mult
    return best


def _dot(a, b):
    return jnp.dot(a, b, preferred_element_type=F32)


def _dot_nt(a, b):
    return lax.dot_general(a, b, NT_DIMS, preferred_element_type=F32)


def _sigmoid(x):
    return 0.5 * jnp.tanh(0.5 * x) + 0.5


def _ada_kernel(cc_ref, w_ref, b_ref, o_ref):
    cc = cc_ref[...]
    s = (cc * _sigmoid(cc)).astype(BF16)
    o_ref[...] = _dot(s, w_ref[...].astype(BF16)) + b_ref[...]


def _ada_all(cc, w_ada, b_ada):
    tn = 1024
    n3 = 3 * D_MODEL
    return pl.pallas_call(
        _ada_kernel,
        grid=(DEPTH, n3 // tn),
        in_specs=[
            pl.BlockSpec((8, D_MODEL), lambda l, j: (0, 0)),
            pl.BlockSpec((None, D_MODEL, tn), lambda l, j: (l, 0, j)),
            pl.BlockSpec((None, 1, tn), lambda l, j: (l, 0, j)),
        ],
        out_specs=pl.BlockSpec((None, 8, tn), lambda l, j: (l, 0, j)),
        out_shape=jax.ShapeDtypeStruct((DEPTH, 8, n3), F32),
        compiler_params=_cparams(("arbitrary", "arbitrary")),
        name="adaln",
    )(cc, w_ada, b_ada.reshape(DEPTH, 1, n3))


def _lb_kernel(x_ref, o_ref):
    x = x_ref[...]
    m = jnp.max(x, axis=0, keepdims=True)
    e = jnp.exp(x - m)
    sm = e / jnp.sum(e, axis=0, keepdims=True)
    rows = []
    run = sm[0:1]
    first = run
    for l in range(DEPTH):
        if l > 0:
            run = run + sm[l:l + 1]
        rows.append(run - first)
    o_ref[...] = jnp.concatenate(rows, axis=0)


def _lower_bounds(hgrn_lb):
    n = 2 * C_HEADS * C_DK
    out = pl.pallas_call(
        _lb_kernel,
        out_shape=jax.ShapeDtypeStruct((DEPTH, n), F32),
        name="hgrn_lb",
    )(hgrn_lb.reshape(DEPTH, n).astype(F32))
    return out.reshape(DEPTH, 2, 1, C_HEADS * C_DK)


def _rms(x, g):
    return x * lax.rsqrt(jnp.mean(x * x, axis=-1, keepdims=True) + EPS) * g


def _mod_rows(mod_ref, is_ctx):
    m = mod_ref[...]
    sel = jnp.where(is_ctx, m[1:2, :], m[0:1, :])
    return sel[:, 0:D_MODEL], sel[:, D_MODEL:2 * D_MODEL], sel[:, 2 * D_MODEL:3 * D_MODEL]


def _prenorm_kernel(xc_ref, xl_ref, g_ref, mod_ref, h_ref, *, ctx_tiles):
    is_ctx = pl.program_id(0) < ctx_tiles
    shift, scale, _ = _mod_rows(mod_ref, is_ctx)
    x = jnp.where(is_ctx, xc_ref[...], xl_ref[...])
    h = _rms(x, g_ref[...]) * (1.0 + scale) + shift
    h_ref[...] = h.astype(BF16)


def _ctx_row(ct):
    return lambda i: (jnp.minimum(i, ct - 1), 0)


def _lat_row(ct):
    return lambda i: (jnp.maximum(i - ct, 0), 0)


def _prenorm(xc, xl, g_pre, mod, li):
    L, T = xc.shape[0], xl.shape[0]
    tm = 256
    ct = L // tm
    return pl.pallas_call(
        functools.partial(_prenorm_kernel, ctx_tiles=ct),
        grid=((L + T) // tm,),
        in_specs=[
            pl.BlockSpec((tm, D_MODEL), _ctx_row(ct)),
            pl.BlockSpec((tm, D_MODEL), _lat_row(ct)),
            pl.BlockSpec((None, 1, D_MODEL), lambda i: (li, 0, 0)),
            pl.BlockSpec((None, 8, 3 * D_MODEL), lambda i: (li, 0, 0)),
        ],
        out_specs=pl.BlockSpec((tm, D_MODEL), lambda i: (i, 0)),
        out_shape=jax.ShapeDtypeStruct((L + T, D_MODEL), BF16),
        compiler_params=_cparams(("arbitrary",)),
        name="prenorm",
    )(xc, xl, g_pre, mod)


def _proj_kernel(h_ref, wt_ref, o_ref, wbf_ref, *, scale):
    @pl.when(pl.program_id(1) == 0)
    def _():
        w = wt_ref[...]
        wbf_ref[...] = (w if scale == 1.0 else w * scale).astype(BF16)

    o_ref[...] = _dot_nt(h_ref[...], wbf_ref[...]).astype(o_ref.dtype)


def _proj(h, w_t, li, start, width, tn, scale=1.0):
    R = h.shape[0]
    tm = _pick_tile(R, 1408)
    assert width % tn == 0 and start % 8 == 0 and start + width <= w_t.shape[1]
    return pl.pallas_call(
        functools.partial(_proj_kernel, scale=scale),
        grid=(width // tn, R // tm),
        in_specs=[
            pl.BlockSpec((tm, D_MODEL), lambda j, i: (i, 0)),
            pl.BlockSpec((None, pl.Element(tn), pl.Element(D_MODEL)),
                         lambda j, i: (li, pl.multiple_of(start + j * tn, 8), 0)),
        ],
        out_specs=pl.BlockSpec((tm, tn), lambda j, i: (i, j)),
        out_shape=jax.ShapeDtypeStruct((R, width), BF16),
        scratch_shapes=[pltpu.VMEM((tn, D_MODEL), BF16)],
        compiler_params=_cparams(("arbitrary", "arbitrary")),
        name="proj",
    )(h, w_t)


def _rope_tile(t, cos, sin):
    lane = lax.broadcasted_iota(jnp.int32, t.shape, 1)
    first = (lane & 16) == 0
    partner = jnp.where(first, pltpu.roll(t, LANES - 16, 1), pltpu.roll(t, 16, 1))
    return t * cos + partner * sin


def _prep_kernel(aq_ref, kv_ref, cq_ref, cos_ref, sin_ref, qn_ref, wuq_ref, kn_ref, wukv_ref,
                 aq_o, ak_o, qc_o, ql_o, k_o, v_o, *, ctx_tiles):
    cos = cos_ref[...]
    sin = sin_ref[...]
    a_scale = A_HEAD_DIM ** -0.5
    for t in range(A_HEADS * A_HEAD_DIM // LANES):
        sl = slice(t * LANES, (t + 1) * LANES)
        aq_o[:, sl] = (_rope_tile(aq_ref[:, sl].astype(F32), cos, sin) * a_scale).astype(BF16)
    ak_o[...] = _rope_tile(kv_ref[:, 0:128].astype(F32), cos, sin).astype(BF16)

    ckv = kv_ref[:, 256:384].astype(F32)
    ckv = _rms(ckv, kn_ref[...])
    kv = _dot(ckv.astype(BF16), wukv_ref[...])
    krt = kv_ref[:, 384:512].astype(F32)
    krt = jnp.where(lax.broadcasted_iota(jnp.int32, krt.shape, 1) < B_ROPE, krt, 0.0)
    kr = _rope_tile(krt, cos, sin).astype(BF16)
    for h in range(B_HEADS):
        k_o[h, :, 0:128] = kv[:, 256 * h:256 * h + 128].astype(BF16)
        k_o[h, :, 128:256] = kr
        v_o[h, :, 0:128] = kv[:, 256 * h + 128:256 * h + 256].astype(BF16)
        v_o[h, :, 128:256] = jnp.ones((kv.shape[0], 128), BF16)

    cq = cq_ref[...].astype(F32)
    cq = jnp.where(lax.broadcasted_iota(jnp.int32, cq.shape, 1) < B_Q_LORA, cq, 0.0)
    cq = cq * lax.rsqrt(jnp.sum(cq * cq, axis=-1, keepdims=True) * (1.0 / B_Q_LORA) + EPS) * qn_ref[...]
    q = _dot(cq.astype(BF16), wuq_ref[...])
    b_scale = (B_NOPE + B_ROPE) ** -0.5 * LOG2E
    qn = [(q[:, 256 * h:256 * h + 128] * b_scale).astype(BF16) for h in range(B_HEADS)]
    qr = [(_rope_tile(q[:, 256 * h + 128:256 * h + 256], cos, sin) * b_scale).astype(BF16) for h in range(B_HEADS)]

    def put(q_o):
        for h in range(B_HEADS):
            q_o[h, :, 0:128] = qn[h]
            q_o[h, :, 128:256] = qr[h]

    is_ctx = pl.program_id(0) < ctx_tiles
    pl.when(is_ctx)(lambda: put(qc_o))
    pl.when(jnp.logical_not(is_ctx))(lambda: put(ql_o))


def _prep(p1, pa, cos, sin, qn, wuq, kn, wukv, li, L):
    R = p1.shape[0]
    tm = 256
    ct = L // tm
    row = lambda i: (i, 0)
    return pl.pallas_call(
        functools.partial(_prep_kernel, ctx_tiles=ct),
        grid=(R // tm,),
        in_specs=[
            pl.BlockSpec((tm, 512), lambda i: (i, PA_AQ // 512)),
            pl.BlockSpec((tm, 512), row),
            pl.BlockSpec((tm, 512), lambda i: (i, PA_CQ // 512)),
            pl.BlockSpec((tm, LANES), row),
            pl.BlockSpec((tm, LANES), row),
            pl.BlockSpec((None, 1, 512), lambda i: (li, 0, 0)),
            pl.BlockSpec((None, 512, 1024), lambda i: (li, 0, 0)),
            pl.BlockSpec((None, 1, B_KV_LORA), lambda i: (li, 0, 0)),
            pl.BlockSpec((None, B_KV_LORA, 1024), lambda i: (li, 0, 0)),
        ],
        out_specs=[
            pl.BlockSpec((tm, 512), row),
            pl.BlockSpec((tm, LANES), row),
            pl.BlockSpec((B_HEADS, tm, 256), lambda i: (0, jnp.minimum(i, ct - 1), 0)),
            pl.BlockSpec((B_HEADS, tm, 256), lambda i: (0, jnp.maximum(i - ct, 0), 0)),
            pl.BlockSpec((B_HEADS, tm, 256), lambda i: (0, i, 0)),
            pl.BlockSpec((B_HEADS, tm, 256), lambda i: (0, i, 0)),
        ],
        out_shape=[
            jax.ShapeDtypeStruct((R, 512), BF16),
            jax.ShapeDtypeStruct((R, LANES), BF16),
            jax.ShapeDtypeStruct((B_HEADS, L, 256), BF16),
            jax.ShapeDtypeStruct((B_HEADS, R - L, 256), BF16),
            jax.ShapeDtypeStruct((B_HEADS, R, 256), BF16),
            jax.ShapeDtypeStruct((B_HEADS, R, 256), BF16),
        ],
        compiler_params=_cparams(("arbitrary",)),
        name="prep",
    )(pa, p1, pa, cos, sin, qn, wuq, kn, wukv)


def _swap_halves(t):
    return jnp.concatenate([t[:, 64:], t[:, :64]], axis=1)


WIN_TM = 2 * BLK


def _win_kernel(sink_ref, q_ref, kp_ref, ks_ref, kn_ref, vp_ref, vs_ref, vn_ref, kc_ref, vc_ref, o_ref,
                *, L, R):
    i = pl.program_id(0)
    k_all = jnp.concatenate([kp_ref[...], ks_ref[...], kn_ref[...]], axis=0)
    v_all = jnp.concatenate([vp_ref[...], vs_ref[...], vn_ref[...]], axis=0)
    kc = kc_ref[...]
    vc = vc_ref[...]
    lo = lax.broadcasted_iota(jnp.int32, (BLK, LANES), 1) < 64
    zero = jnp.zeros((BLK, LANES), BF16)
    rows = 4 * BLK
    rr = lax.broadcasted_iota(jnp.int32, (rows, 3 * BLK), 0)
    cc = lax.broadcasted_iota(jnp.int32, (rows, 3 * BLK), 1)
    hrow = lax.broadcasted_iota(jnp.int32, (rows, 1), 0) // BLK

    for sb in range(WIN_TM // BLK):
        bi = i * (WIN_TM // BLK) + sb
        q = q_ref[sb * BLK:(sb + 1) * BLK, :]
        kl = k_all[sb * BLK:(sb + 3) * BLK]
        vl = v_all[sb * BLK:(sb + 3) * BLK]
        qpos = (rr & (BLK - 1)) + bi * BLK
        kpos = cc + (bi - 1) * BLK
        ok = ((kpos >= L) & (kpos < R) & (qpos >= L)
              & (kpos - qpos <= WINDOW) & (qpos - kpos <= WINDOW))
        for g in range(2):
            t0 = q[:, 256 * g:256 * g + 128]
            t1 = q[:, 256 * g + 128:256 * g + 256]
            if g == 0:
                parts = [jnp.where(lo, t0, zero), jnp.where(lo, _swap_halves(t0), zero),
                         jnp.where(lo, t1, zero), jnp.where(lo, _swap_halves(t1), zero)]
            else:
                parts = [jnp.where(lo, zero, _swap_halves(t0)), jnp.where(lo, zero, t0),
                         jnp.where(lo, zero, _swap_halves(t1)), jnp.where(lo, zero, t1)]
            qs = jnp.concatenate(parts, axis=0)
            s_l = jnp.where(ok, _dot_nt(qs, kl), -jnp.inf)
            s_c = _dot_nt(qs, kc)
            sink = jnp.zeros((rows, 1), F32)
            for j in range(4):
                sink = jnp.where(hrow == j, sink_ref[4 * g + j], sink)
            m = jnp.maximum(jnp.maximum(jnp.max(s_l, axis=-1, keepdims=True),
                                        jnp.max(s_c, axis=-1, keepdims=True)), sink)
            p_l = jnp.exp(s_l - m)
            p_c = jnp.exp(s_c - m)
            den = (jnp.sum(p_l, axis=-1, keepdims=True) + jnp.sum(p_c, axis=-1, keepdims=True)
                   + jnp.exp(sink - m))
            o = (_dot(p_l.astype(BF16), vl) + _dot(p_c.astype(BF16), vc)) / den
            for pair in range(2):
                a = o[2 * pair * BLK:(2 * pair + 1) * BLK]
                b = o[(2 * pair + 1) * BLK:(2 * pair + 2) * BLK]
                if g == 0:
                    tile = jnp.where(lo, a, _swap_halves(b))
                else:
                    tile = jnp.where(lo, _swap_halves(a), b)
                c0 = 256 * g + 128 * pair
                o_ref[sb * BLK:(sb + 1) * BLK, c0:c0 + 128] = tile.astype(BF16)


def _window_attn(sink, aq, ak, p, L):
    R = aq.shape[0]
    nb = R // BLK
    per = WIN_TM // BLK
    prev = lambda c: (lambda i: (jnp.maximum(i * per - 1, 0), c))
    this = lambda c: (lambda i: (i, c))
    nxt = lambda c: (lambda i: (jnp.minimum((i + 1) * per, nb - 1), c))
    return pl.pallas_call(
        functools.partial(_win_kernel, L=L, R=R),
        grid=(R // WIN_TM,),
        in_specs=[
            pl.BlockSpec(memory_space=pltpu.SMEM),
            pl.BlockSpec((WIN_TM, 512), lambda i: (i, 0)),
            pl.BlockSpec((BLK, LANES), prev(0)),
            pl.BlockSpec((WIN_TM, LANES), this(0)),
            pl.BlockSpec((BLK, LANES), nxt(0)),
            pl.BlockSpec((BLK, LANES), prev(1)),
            pl.BlockSpec((WIN_TM, LANES), this(1)),
            pl.BlockSpec((BLK, LANES), nxt(1)),
            pl.BlockSpec((L, LANES), lambda i: (0, 0)),
            pl.BlockSpec((L, LANES), lambda i: (0, 1)),
        ],
        out_specs=pl.BlockSpec((WIN_TM, 512), lambda i: (i, 0)),
        out_shape=jax.ShapeDtypeStruct((R, 512), BF16),
        compiler_params=_cparams(("arbitrary",)),
        name="window_attn",
    )(sink, aq, ak, ak, ak, p, p, p, ak, p)


def _softmax_out(acc):
    return (acc[:, 0:B_VDIM] / acc[:, B_VDIM:2 * B_VDIM]).astype(BF16)


def _mla_kernel(ql_ref, qc_ref, k_ref, v_ref, ol_ref, oc_ref, s0_ref, s1_ref, m0_ref, m1_ref, *, L, tq, tk):
    i = pl.program_id(1)
    nk = k_ref.shape[0] // tk

    @pl.when(i == 0)
    def _():
        s1_ref[...] = jnp.zeros_like(s1_ref)
        m1_ref[...] = jnp.zeros_like(m1_ref)

    def step(sw_ref, mw_ref, sr_ref, mr_ref):
        q = ql_ref[...]
        m_prev = mr_ref[...]
        mrun = jnp.full((tq, LANES), -jnp.inf, F32)
        acc = jnp.zeros((tq, 2 * B_VDIM), F32)
        for c in range(nk):
            rows = slice(c * tk, (c + 1) * tk)
            s = _dot_nt(q, k_ref[rows, :])
            sw_ref[c] = s
            for t in range(tk // LANES):
                mrun = jnp.maximum(mrun, s[:, t * LANES:(t + 1) * LANES])
            sp = sr_ref[c]
            p = jnp.concatenate(
                [jnp.exp2(sp[:, t * LANES:(t + 1) * LANES] - m_prev) for t in range(tk // LANES)], axis=1)
            acc = acc + _dot(p.astype(BF16), v_ref[rows, :])
        mw_ref[...] = jnp.broadcast_to(jnp.max(mrun, axis=-1, keepdims=True), (tq, LANES))
        ol_ref[...] = _softmax_out(acc)

    @pl.when(i % 2 == 0)
    def _():
        step(s0_ref, m0_ref, s1_ref, m1_ref)

    @pl.when(i % 2 == 1)
    def _():
        step(s1_ref, m1_ref, s0_ref, m0_ref)

    @pl.when(i == 0)
    def _():
        s = _dot_nt(qc_ref[...], k_ref[0:L, :])
        p = jnp.exp2(s - jnp.max(s, axis=-1, keepdims=True))
        oc_ref[...] = _softmax_out(_dot(p.astype(BF16), v_ref[0:L, :]))


def _mla_attn(q_ctx, q_lat, k, v):
    L, T, R = q_ctx.shape[1], q_lat.shape[1], k.shape[1]
    tq = _pick_tile(T, 512, 256)
    tk = 256
    nt = T // tq
    once = pl.Buffered(1)
    return pl.pallas_call(
        functools.partial(_mla_kernel, L=L, tq=tq, tk=tk),
        grid=(B_HEADS, nt + 1),
        in_specs=[
            pl.BlockSpec((None, tq, 256), lambda h, i: (h, jnp.minimum(i, nt - 1), 0)),
            pl.BlockSpec((None, L, 256), lambda h, i: (h, 0, 0)),
            pl.BlockSpec((None, R, 256), lambda h, i: (h, 0, 0), pipeline_mode=once),
            pl.BlockSpec((None, R, 256), lambda h, i: (h, 0, 0), pipeline_mode=once),
        ],
        out_specs=[
            pl.BlockSpec((tq, B_VDIM), lambda h, i: (jnp.maximum(i - 1, 0), h)),
            pl.BlockSpec((L, B_VDIM), lambda h, i: (0, h)),
        ],
        out_shape=[
            jax.ShapeDtypeStruct((T, B_HEADS * B_VDIM), BF16),
            jax.ShapeDtypeStruct((L, B_HEADS * B_VDIM), BF16),
        ],
        scratch_shapes=[
            pltpu.VMEM((R // tk, tq, tk), F32),
            pltpu.VMEM((R // tk, tq, tk), F32),
            pltpu.VMEM((tq, LANES), F32),
            pltpu.VMEM((tq, LANES), F32),
        ],
        compiler_params=_cparams(("arbitrary", "arbitrary")),
        name="mla_attn",
    )(q_lat, q_ctx, k, v)


LEVELS = (64, 32, 16, 8, 4, 2, 1)


def _split3(x):
    hi = x.astype(BF16)
    r = x - hi.astype(F32)
    mid = r.astype(BF16)
    lo = (r - mid.astype(F32)).astype(BF16)
    return hi, mid, lo


def _hgrn_kernel(zf_ref, vf_ref, qf_ref, zb_ref, vb_ref, qb_ref, lb_ref, of_ref, ob_ref, st_ref):
    n = pl.program_id(0)

    @pl.when(n == 0)
    def _():
        st_ref[...] = jnp.zeros_like(st_ref)

    half = C_HEADS * C_DK
    width = 2 * half
    nh = 2 * C_HEADS
    heads = [slice(h * C_DK, (h + 1) * C_DK) for h in range(nh)]
    r0 = lax.broadcasted_iota(jnp.int32, (BLK, BLK), 0)
    c0 = lax.broadcasted_iota(jnp.int32, (BLK, BLK), 1)
    lt = (r0 >= c0).astype(BF16)
    rev = (r0 + c0 == BLK - 1).astype(BF16)
    ones = jnp.ones((BLK, BLK), BF16)
    sub8 = lax.broadcasted_iota(jnp.int32, (BLK // 8, 8, width), 1)

    z = jnp.concatenate([zf_ref[...].astype(F32), _dot(rev, zb_ref[...])], axis=1)
    v = jnp.concatenate([vf_ref[...].astype(F32), _dot(rev, vb_ref[...])], axis=1)
    vb = v.astype(BF16)
    qp = jnp.concatenate([qf_ref[...].astype(F32), _dot(rev, qb_ref[...])], axis=1)
    lb = jnp.concatenate([lb_ref[0], lb_ref[1]], axis=1)

    ls = jnp.minimum(z, 0.0) - jnp.log(1.0 + jnp.exp(-jnp.abs(z)))
    a1 = jnp.log(lb)
    a2 = jnp.log1p(-lb) + ls
    lf = jnp.maximum(a1, a2) + jnp.log(1.0 + jnp.exp(-jnp.abs(a1 - a2)))
    k = (1.0 - lb) * (1.0 / (1.0 + jnp.exp(z)))
    q = qp * _sigmoid(qp) * (C_DK ** -0.5)

    hi, mid, lo = _split3(lf * LOG2E)
    b = _dot(lt, hi) + _dot(lt, mid) + _dot(lt, lo)

    sts = [st_ref[h] for h in range(nh)]
    qe = (q * jnp.exp2(b)).astype(BF16)
    outs = [_dot_nt(qe[:, sl], st.astype(BF16)) for sl, st in zip(heads, sts)]

    b8 = b.reshape(BLK // 8, 8, width)
    atts = [jnp.zeros((BLK, BLK), F32) for _ in heads]
    for s in LEVELS:
        if s >= 8:
            g = BLK // (2 * s)
            e = jnp.broadcast_to(b.reshape(g, 2 * s, width)[:, s - 1:s, :], (g, 2 * s, width))
        elif s == 4:
            e = jnp.broadcast_to(b8[:, 3:4, :], b8.shape)
        elif s == 2:
            e = jnp.where(sub8 < 4, jnp.broadcast_to(b8[:, 1:2, :], b8.shape),
                          jnp.broadcast_to(b8[:, 5:6, :], b8.shape))
        else:
            e = jnp.where((sub8 & 1) == 0, b8, pltpu.roll(b8, 1, 1))
        w = jnp.exp2(-jnp.abs(b - e.reshape(BLK, width)))
        qw = q * w
        kw = k * w
        right = (r0 & s) != 0
        pair = (r0 // (2 * s)) == (c0 // (2 * s))
        atts = [jnp.where(pair,
                          _dot_nt(jnp.where(right, qw[:, sl], 0.0).astype(BF16),
                                  jnp.where(right, 0.0, kw[:, sl]).astype(BF16)), att)
                for sl, att in zip(heads, atts)]
    dg = (q * k).astype(BF16)
    outs = [o + _dot(att.astype(BF16), vb[:, sl]) + _dot(dg[:, sl], ones) * v[:, sl]
            for sl, o, att in zip(heads, outs, atts)]

    bl = b[BLK - 1:BLK, :]
    kl = (k * jnp.exp2(bl - b)).astype(BF16)
    dec = jnp.exp2(bl)
    for h, sl in enumerate(heads):
        st_ref[h] = sts[h] * dec[:, sl] + _dot(v[:, sl].T.astype(BF16), kl[:, sl])

    of_ref[...] = jnp.concatenate(outs[:C_HEADS], axis=1)
    o = jnp.concatenate(outs[C_HEADS:], axis=1)
    ohi = o.astype(BF16)
    olo = (o - ohi.astype(F32)).astype(BF16)
    ob_ref[...] = _dot(rev, ohi) + _dot(rev, olo)


def _hgrn(pa, pc, lb, li, L):
    R = pa.shape[0]
    nb = R // BLK
    nc = L // BLK

    def back(n):
        return jnp.where(n < nc, nc - 1 - n, nb - 1 - (n - nc))

    fwd = lambda c: (lambda n: (n, c))
    bwd = lambda c: (lambda n: (back(n), c))
    return pl.pallas_call(
        _hgrn_kernel,
        grid=(nb,),
        in_specs=[
            pl.BlockSpec((BLK, 512), fwd(PA_CFF // 512)),
            pl.BlockSpec((BLK, 512), fwd(PA_CI // 512)),
            pl.BlockSpec((BLK, 512), fwd(PC_HQ // 512)),
            pl.BlockSpec((BLK, 512), bwd(PA_CFF // 512 + 1)),
            pl.BlockSpec((BLK, 512), bwd(PA_CI // 512)),
            pl.BlockSpec((BLK, 512), bwd(PC_HQ // 512)),
            pl.BlockSpec((None, 2, 1, 512), lambda n: (li, 0, 0, 0)),
        ],
        out_specs=[pl.BlockSpec((BLK, 512), fwd(0)), pl.BlockSpec((BLK, 512), bwd(0))],
        out_shape=[jax.ShapeDtypeStruct((R, 512), F32), jax.ShapeDtypeStruct((R, 512), F32)],
        scratch_shapes=[pltpu.VMEM((2 * C_HEADS, C_DK, C_DK), F32)],
        compiler_params=_cparams(("arbitrary",)),
        name="hgrn",
    )(pa, pa, pc, pa, pa, pc, lb)


POOL_TM = 256
POOL_HALO = 16


def _pool_kernel(up_ref, us_ref, un_ref, w_ref, sc_ref, o_ref, *, L, R):
    i = pl.program_id(0)
    tm, halo = POOL_TM, POOL_HALO
    u3 = jnp.concatenate([up_ref[...], us_ref[...], un_ref[...]], axis=0)
    us = us_ref[...].astype(F32)
    r = lax.broadcasted_iota(jnp.int32, (tm, tm + 2 * halo), 0) + i * tm
    s = lax.broadcasted_iota(jnp.int32, (tm, tm + 2 * halo), 1) + (i * tm - halo)
    seq_lo = jnp.where(r < L, 0, L)
    seq_hi = jnp.where(r < L, L, R)
    r1 = lax.broadcasted_iota(jnp.int32, (tm, LANES), 0) + i * tm
    seq_lo1 = jnp.where(r1 < L, 0, L)
    seq_hi1 = jnp.where(r1 < L, L, R)
    for g, w in enumerate(POOL_WINDOWS):
        sl = slice(g * LANES, (g + 1) * LANES)
        lo = jnp.maximum(r - w // 2, seq_lo)
        hi = jnp.minimum(r + w - w // 2, seq_hi)
        band = ((s >= lo) & (s < hi)).astype(BF16)
        cnt = (jnp.minimum(r1 + w - w // 2, seq_hi1) - jnp.maximum(r1 - w // 2, seq_lo1)).astype(F32)
        mean = _dot(band, u3[:, sl]) / cnt
        d = (mean - us[:, sl]).astype(BF16)
        o_ref[:, sl] = (_dot(d, w_ref[g]) * sc_ref[:, sl]).astype(BF16)


def _pool(pc, w_pool, pool_scale, li, L):
    R = pc.shape[0]
    tm, halo = POOL_TM, POOL_HALO
    assert max(POOL_WINDOWS) // 2 <= halo and tm % halo == 0
    nb = R // tm
    hb = tm // halo
    cb = PC_DX // 512
    return pl.pallas_call(
        functools.partial(_pool_kernel, L=L, R=R),
        grid=(nb,),
        in_specs=[
            pl.BlockSpec((halo, 512), lambda i: (jnp.maximum(i * hb - 1, 0), cb)),
            pl.BlockSpec((tm, 512), lambda i: (i, cb)),
            pl.BlockSpec((halo, 512), lambda i: (jnp.minimum((i + 1) * hb, R // halo - 1), cb)),
            pl.BlockSpec((None, len(POOL_WINDOWS), LANES, LANES), lambda i: (li, 0, 0, 0)),
            pl.BlockSpec((None, 1, 512), lambda i: (li, 0, 0)),
        ],
        out_specs=pl.BlockSpec((tm, 512), lambda i: (i, 0)),
        out_shape=jax.ShapeDtypeStruct((R, 512), BF16),
        compiler_params=_cparams(("arbitrary",)),
        name="pool",
    )(pc, pc, pc, w_pool, pool_scale)


def _merge_kernel(oa_ref, obc_ref, obl_ref, ohf_ref, ohb_ref, od_ref, gt_ref, mg0_ref, mg1_ref, mg2_ref, mg3_ref,
                  wbr_ref, wout_ref, hn_ref, xc_ref, xl_ref, mod_ref, gpost_ref, gnext_ref, modn_ref,
                  xo_ref, *ho_refs, ctx_tiles):
    is_ctx = pl.program_id(0) < ctx_tiles
    hs = ohf_ref[...] + ohb_ref[...]
    hn = hn_ref[...]
    oh = jnp.concatenate(
        [_rms(hs[:, h * C_DK:(h + 1) * C_DK], hn[:, h * C_DK:(h + 1) * C_DK]) for h in range(C_HEADS)], axis=1)
    ob = jnp.where(is_ctx, obc_ref[...], obl_ref[...])
    outs = (oa_ref[...].astype(F32), ob.astype(F32), oh, od_ref[...].astype(F32))
    mgs = (mg0_ref, mg1_ref, mg2_ref, mg3_ref)
    merged = None
    for nbr in range(N_BRANCH):
        u = gt_ref[:, nbr * BRANCH_W:(nbr + 1) * BRANCH_W].astype(F32)
        ys = (outs[nbr] * (u * (jnp.tanh(u) + 1.0))).astype(BF16)
        yb = _dot(ys, wbr_ref[nbr])
        term = (jnp.tanh(mgs[nbr][...].astype(F32)) + 1.0) * yb
        merged = term if merged is None else merged + term
    y = _dot((merged * 0.5).astype(BF16), wout_ref[...])
    _, _, gate_r = _mod_rows(mod_ref, is_ctx)
    xn = jnp.where(is_ctx, xc_ref[...], xl_ref[...]) + gate_r * _rms(y, gpost_ref[...])
    xo_ref[...] = xn
    for ho_ref in ho_refs:
        shift_n, scale_n, _ = _mod_rows(modn_ref, is_ctx)
        ho_ref[...] = (_rms(xn, gnext_ref[...]) * (1.0 + scale_n) + shift_n).astype(BF16)


def row_all(i):
    return (i, 0)


def _merge(oa, obc, obl, ohf, ohb, od, pb, wbr, wout, hnorm, xc, xl, mod, g_post, g_pre, li, L):
    R = oa.shape[0]
    tm = 256
    ct = L // tm
    xl_row = _lat_row(ct) if xl.shape[0] == R - L else row_all
    last = li == DEPTH - 1
    ln = min(li + 1, DEPTH - 1)
    row = row_all
    gcol = 0
    mcol = 1
    if last:
        ct = L // tm
        out_specs = [pl.BlockSpec((tm, D_MODEL), lambda i: (jnp.maximum(i - ct, 0), 0))]
        out_shape = [jax.ShapeDtypeStruct((R - L, D_MODEL), F32)]
    else:
        out_specs = [pl.BlockSpec((tm, D_MODEL), row), pl.BlockSpec((tm, D_MODEL), row)]
        out_shape = [jax.ShapeDtypeStruct((R, D_MODEL), F32), jax.ShapeDtypeStruct((R, D_MODEL), BF16)]
    return pl.pallas_call(
        functools.partial(_merge_kernel, ctx_tiles=L // tm),
        grid=(R // tm,),
        in_specs=[
            pl.BlockSpec((tm, 512), row),
            pl.BlockSpec((tm, 512), _ctx_row(ct)),
            pl.BlockSpec((tm, 512), _lat_row(ct)),
            pl.BlockSpec((tm, 512), row),
            pl.BlockSpec((tm, 512), row),
            pl.BlockSpec((tm, 512), row),
            pl.BlockSpec((tm, D_MODEL), lambda i: (i, gcol)),
            pl.BlockSpec((tm, D_MODEL), lambda i: (i, mcol)),
            pl.BlockSpec((tm, D_MODEL), lambda i: (i, mcol + 1)),
            pl.BlockSpec((tm, D_MODEL), lambda i: (i, mcol + 2)),
            pl.BlockSpec((tm, D_MODEL), lambda i: (i, mcol + 3)),
            pl.BlockSpec((None, N_BRANCH, BRANCH_W, D_MODEL), lambda i: (li, 0, 0, 0)),
            pl.BlockSpec((None, D_MODEL, D_MODEL), lambda i: (li, 0, 0)),
            pl.BlockSpec((None, 1, 512), lambda i: (li, 0, 0)),
            pl.BlockSpec((tm, D_MODEL), _ctx_row(ct)),
            pl.BlockSpec((tm, D_MODEL), xl_row),
            pl.BlockSpec((None, 8, 3 * D_MODEL), lambda i: (li, 0, 0)),
            pl.BlockSpec((None, 1, D_MODEL), lambda i: (li, 0, 0)),
            pl.BlockSpec((None, 1, D_MODEL), lambda i: (ln, 0, 0)),
            pl.BlockSpec((None, 8, 3 * D_MODEL), lambda i: (ln, 0, 0)),
        ],
        out_specs=out_specs,
        out_shape=out_shape,
        compiler_params=_cparams(("arbitrary",)),
        name="merge",
    )(oa, obc, obl, ohf, ohb, od, pb, pb, pb, pb, pb, wbr, wout, hnorm, xc, xl, mod, g_post, g_pre, mod)


def _rope_tables(T, L):
    rows = jnp.repeat(jnp.arange(T // GRID_W, dtype=jnp.int32), GRID_W)
    cols = jnp.tile(jnp.arange(GRID_W, dtype=jnp.int32), T // GRID_W)
    half = 16
    freqs = ROPE_BASE ** (-jnp.arange(half, dtype=F32) / half)
    ar = rows.astype(F32)[:, None] * freqs[None, :]
    ac = cols.astype(F32)[:, None] * freqs[None, :]
    cos = jnp.concatenate([jnp.cos(ar), jnp.cos(ar), jnp.cos(ac), jnp.cos(ac)], axis=1)
    sin = jnp.concatenate([-jnp.sin(ar), jnp.sin(ar), -jnp.sin(ac), jnp.sin(ac)], axis=1)
    cos = jnp.concatenate([jnp.ones((L, 64), F32), cos], axis=0)
    sin = jnp.concatenate([jnp.zeros((L, 64), F32), sin], axis=0)
    return jnp.tile(cos, (1, 2)), jnp.tile(sin, (1, 2))


def kernel(x, c, ctx, c_ctx, w_ada, b_ada, g_pre, g_post, w_in, a_sink, mla_q_norm, w_uq, mla_kv_norm,
           w_ukv, hgrn_lb, hgrn_norm, w_pool, pool_scale, w_branch, w_out):
    assert x.shape[0] == 1 and ctx.shape[0] == 1
    T = x.shape[1]
    L = ctx.shape[1]
    assert T % 256 == 0 and L % 256 == 0 and T % GRID_W == 0

    w_t = jnp.swapaxes(w_in.astype(F32), 1, 2)
    wuq = w_uq.astype(BF16).reshape(DEPTH, B_Q_LORA, B_HEADS, B_NOPE + B_ROPE)
    wuq = jnp.pad(wuq, ((0, 0), (0, 512 - B_Q_LORA), (0, 0), (0, 256 - B_NOPE - B_ROPE))).reshape(DEPTH, 512, 1024)
    qn = jnp.pad(mla_q_norm.astype(F32), ((0, 0), (0, 512 - B_Q_LORA))).reshape(DEPTH, 1, 512)
    kn = mla_kv_norm.astype(F32).reshape(DEPTH, 1, B_KV_LORA)
    wukv = w_ukv.astype(BF16)
    wpool = w_pool.astype(BF16)
    pscale = pool_scale.astype(F32).reshape(DEPTH, 1, BRANCH_W)
    wbr = w_branch.astype(BF16)
    wout = w_out.astype(BF16)
    hnorm = hgrn_norm.astype(F32).reshape(DEPTH, 1, C_HEADS * C_DK)
    gpre = g_pre.astype(F32).reshape(DEPTH, 1, D_MODEL)
    gpost = g_post.astype(F32).reshape(DEPTH, 1, D_MODEL)

    cos, sin = _rope_tables(T, L)

    cc = jnp.concatenate([c.astype(F32), c_ctx.astype(F32)[None, :], jnp.zeros((6, D_MODEL), F32)], axis=0)
    mod = _ada_all(cc, w_ada, b_ada)
    lb = _lower_bounds(hgrn_lb)

    xc, xl = ctx[0].astype(F32), x[0].astype(F32)
    h = _prenorm(xc, xl, gpre, mod, 0)
    for li in range(DEPTH):
        p1 = _proj(h, w_t, li, 0, O_P1_END, 512)
        pa = _proj(h, w_t, li, O_PA, PA_WIDTH, 1280)
        pc = _proj(h, w_t, li, O_PC, O_PB - O_PC, 1024)
        pb = _proj(h, w_t, li, O_PB, O_END - O_PB, 1024, HALF_GATES)
        aq, ak, mqc, mql, mk, mv = _prep(p1, pa, cos, sin, qn, wuq, kn, wukv, li, L)
        oa = _window_attn(a_sink[li].astype(F32), aq, ak, p1, L)
        obl, obc = _mla_attn(mqc, mql, mk, mv)
        ohf, ohb = _hgrn(pa, pc, lb, li, L)
        od = _pool(pc, wpool, pscale, li, L)
        outs = _merge(oa, obc, obl, ohf, ohb, od, pb, wbr, wout, hnorm, xc, xl, mod, gpost, gpre, li, L)
        xc = xl = outs[0]
        h = outs[1] if len(outs) == 2 else None
    return xl[None].astype(x.dtype)
```
